```python
import jax, jax.numpy as jnp
from jax import lax
import numpy as np

D_MODEL = 2048
BATCH = 4
SEQ = 4096
DEPTH = 1

CONV_WIDTH = 1024
CONV_KERNEL = 31
N_HEADS = 8
N_KV_HEADS = 2
HEAD_DIM = 128
ATTN_WIDTH = N_HEADS * HEAD_DIM
KV_WIDTH = N_KV_HEADS * HEAD_DIM
IDX_HEADS = 16
IDX_DIM = 64
TOPK_MAX = 256
ROPE_THETA = 500000.0
ROPE_FRACTION_DIV = 4
D_FF = 5632
Q_BLOCK = 128
NORM_EPS = 1e-6
N_BRANCHES = 2
IN_SPLITS = (2 * CONV_WIDTH, ATTN_WIDTH, KV_WIDTH, KV_WIDTH, IDX_HEADS * IDX_DIM, IDX_DIM, IDX_HEADS, N_BRANCHES * D_MODEL)
IN_WIDTH = 2 * CONV_WIDTH + ATTN_WIDTH + 2 * KV_WIDTH + IDX_HEADS * IDX_DIM + IDX_DIM + IDX_HEADS + N_BRANCHES * D_MODEL

kernel_name = "hybrid_gated_conv_dsa_macaron"


def rms_norm(x, g):
    xf = x.astype(jnp.float32)
    y = xf * lax.rsqrt(jnp.mean(xf * xf, axis=-1, keepdims=True) + NORM_EPS)
    return (y * g.astype(jnp.float32)).astype(x.dtype)


def layer_norm(x, g, b):
    xf = x.astype(jnp.float32)
    mu = jnp.mean(xf, axis=-1, keepdims=True)
    xc = xf - mu
    y = xc * lax.rsqrt(jnp.mean(xc * xc, axis=-1, keepdims=True) + NORM_EPS)
    return (y * g.astype(jnp.float32) + b.astype(jnp.float32)).astype(x.dtype)


def swiglu_ffn(x, w1, w2):
    gate, up = jnp.split(x @ w1, 2, axis=-1)
    return (jax.nn.silu(gate) * up) @ w2


def partial_rope(x, positions):
    dh = x.shape[-1]
    r = dh // ROPE_FRACTION_DIV
    half = r // 2
    inv_freq = jnp.power(jnp.float32(ROPE_THETA), -jnp.arange(half, dtype=jnp.float32) * (2.0 / r))
    ang = positions.astype(jnp.float32)[..., None] * inv_freq
    cos = jnp.cos(ang)[:, :, None, :].astype(x.dtype)
    sin = jnp.sin(ang)[:, :, None, :].astype(x.dtype)
    x1 = x[..., :half]
    x2 = x[..., half:r]
    return jnp.concatenate([x1 * cos - x2 * sin, x2 * cos + x1 * sin, x[..., r:]], axis=-1)


def split_columns(z):
    bounds = np.cumsum(np.array(IN_SPLITS))[:-1].tolist()
    return jnp.split(z, bounds, axis=-1)


def conformer_conv(z_glu, dw, dw_b, ln_g, ln_b, w_pw):
    a, b = jnp.split(z_glu, 2, axis=-1)
    u = a * jax.nn.sigmoid(b)
    u = lax.conv_general_dilated(
        u, dw[:, None, :].astype(u.dtype), window_strides=(1,), padding=[(CONV_KERNEL - 1, 0)],
        dimension_numbers=("NWC", "WIO", "NWC"), feature_group_count=CONV_WIDTH) + dw_b
    u = jax.nn.silu(layer_norm(u, ln_g, ln_b))
    return u @ w_pw


def dsa_attention(q, k, v, qi, ki, wi):
    B, T = q.shape[0], q.shape[1]
    topk = min(TOPK_MAX, T // 4)
    n_blocks = T // Q_BLOCK
    key_pos = jnp.arange(T, dtype=jnp.int32)
    qg = q.reshape(B, T, N_KV_HEADS, N_HEADS // N_KV_HEADS, HEAD_DIM)
    gather_rows = jax.vmap(lambda table, idx: table[idx])
    idx_scale = IDX_DIM ** -0.5
    attn_scale = HEAD_DIM ** -0.5

    def block(i):
        start = i * Q_BLOCK
        q_pos = start + jnp.arange(Q_BLOCK, dtype=jnp.int32)
        qb = lax.dynamic_slice_in_dim(qg, start, Q_BLOCK, axis=1)
        qib = lax.dynamic_slice_in_dim(qi, start, Q_BLOCK, axis=1)
        wib = lax.dynamic_slice_in_dim(wi, start, Q_BLOCK, axis=1)
        dots = jnp.einsum("bqhd,bsd->bqhs", qib, ki).astype(jnp.float32) * idx_scale
        scores = jnp.einsum("bqh,bqhs->bqs", wib.astype(jnp.float32), jax.nn.relu(dots))
        causal = key_pos[None, :] <= q_pos[:, None]
        scores = jnp.where(causal[None], scores, -jnp.inf)
        _, sel = lax.top_k(scores, topk)
        ks = gather_rows(k, sel)
        vs = gather_rows(v, sel)
        logits = jnp.einsum("bqgrd,bqkgd->bqgrk", qb, ks).astype(jnp.float32) * attn_scale
        valid = sel <= q_pos[None, :, None]
        logits = jnp.where(valid[:, :, None, None, :], logits, -jnp.inf)
        p = jax.nn.softmax(logits, axis=-1).astype(vs.dtype)
        o = jnp.einsum("bqgrk,bqkgd->bqgrd", p, vs)
        return o.reshape(B, Q_BLOCK, ATTN_WIDTH)

    out = lax.map(block, jnp.arange(n_blocks, dtype=jnp.int32))
    return out.transpose(1, 0, 2, 3).reshape(B, T, ATTN_WIDTH)


def setup_inputs(seed: int = 0) -> dict:
    key = jax.random.key(seed)
    ks = jax.random.split(key, 24)
    L = DEPTH

    def w(k, shape, fan_in):
        return jax.random.normal(k, shape, jnp.float32) * (fan_in ** -0.5)

    def gain(k, n):
        return 1.0 + 0.02 * jax.random.normal(k, (L, n), jnp.float32)

    def bias(k, n):
        return 0.02 * jax.random.normal(k, (L, n), jnp.float32)

    x = jax.random.normal(ks[0], (BATCH, SEQ, D_MODEL), jnp.float32)
    offs = jax.random.randint(ks[1], (BATCH, 1), 0, 1024, dtype=jnp.int32)
    positions = (jnp.arange(SEQ, dtype=jnp.int32)[None, :] + offs).astype(jnp.int32)
    return {
        "x": x,
        "positions": positions,
        "ffn1_norm_pre": gain(ks[2], D_MODEL),
        "ffn1_w1": w(ks[3], (L, D_MODEL, 2 * D_FF), D_MODEL),
        "ffn1_w2": w(ks[4], (L, D_FF, D_MODEL), D_FF),
        "ffn1_norm_post": gain(ks[5], D_MODEL),
        "mix_norm_pre": gain(ks[6], D_MODEL),
        "w_in": w(ks[7], (L, D_MODEL, IN_WIDTH), D_MODEL),
        "conv_dw": w(ks[8], (L, CONV_KERNEL, CONV_WIDTH), CONV_KERNEL),
        "conv_dw_b": bias(ks[9], CONV_WIDTH),
        "conv_ln_g": gain(ks[10], CONV_WIDTH),
        "conv_ln_b": bias(ks[11], CONV_WIDTH),
        "conv_w_pw": w(ks[12], (L, CONV_WIDTH, D_MODEL), CONV_WIDTH),
        "attn_w_o": w(ks[13], (L, ATTN_WIDTH, D_MODEL), ATTN_WIDTH),
        "w_out": w(ks[14], (L, D_MODEL, D_MODEL), D_MODEL),
        "mix_norm_post": gain(ks[15], D_MODEL),
        "ffn2_norm_pre": gain(ks[16], D_MODEL),
        "ffn2_w1": w(ks[17], (L, D_MODEL, 2 * D_FF), D_MODEL),
        "ffn2_w2": w(ks[18], (L, D_FF, D_MODEL), D_FF),
        "ffn2_norm_post": gain(ks[19], D_MODEL),
    }


def reference(x, positions, ffn1_norm_pre, ffn1_w1, ffn1_w2, ffn1_norm_post, mix_norm_pre, w_in,
              conv_dw, conv_dw_b, conv_ln_g, conv_ln_b, conv_w_pw, attn_w_o, w_out, mix_norm_post,
              ffn2_norm_pre, ffn2_w1, ffn2_w2, ffn2_norm_post):
    B, T, _ = x.shape
    for l in range(DEPTH):
        x = x + 0.5 * rms_norm(swiglu_ffn(rms_norm(x, ffn1_norm_pre[l]), ffn1_w1[l], ffn1_w2[l]), ffn1_norm_post[l])

        h = rms_norm(x, mix_norm_pre[l])
        z = h @ w_in[l]
        z_conv, z_q, z_k, z_v, z_qi, z_ki, z_wi, z_gate = split_columns(z)

        y_conv = conformer_conv(z_conv, conv_dw[l], conv_dw_b[l], conv_ln_g[l], conv_ln_b[l], conv_w_pw[l])

        q = partial_rope(z_q.reshape(B, T, N_HEADS, HEAD_DIM), positions)
        k = partial_rope(z_k.reshape(B, T, N_KV_HEADS, HEAD_DIM), positions)
        v = z_v.reshape(B, T, N_KV_HEADS, HEAD_DIM)
        qi = partial_rope(z_qi.reshape(B, T, IDX_HEADS, IDX_DIM), positions)
        ki = partial_rope(z_ki.reshape(B, T, 1, IDX_DIM), positions)[:, :, 0, :]
        wi = z_wi * (IDX_HEADS ** -0.5)
        y_attn = dsa_attention(q, k, v, qi, ki, wi) @ attn_w_o[l]

        gates = jax.nn.sigmoid(z_gate.reshape(B, T, N_BRANCHES, D_MODEL))
        merged = gates[:, :, 0, :] * y_conv + gates[:, :, 1, :] * y_attn
        x = x + rms_norm(merged @ w_out[l], mix_norm_post[l])

        x = x + 0.5 * rms_norm(swiglu_ffn(rms_norm(x, ffn2_norm_pre[l]), ffn2_w1[l], ffn2_w2[l]), ffn2_norm_post[l])
    return x
```

```python
import functools
import math

import jax
import jax.numpy as jnp
from jax import lax
from jax.experimental import pallas as pl
from jax.experimental.pallas import tpu as pltpu

F32 = jnp.float32
BF16 = jnp.bfloat16

CONV_WIDTH = 1024
CONV_KERNEL = 31
N_HEADS = 8
N_KV_HEADS = 2
HEAD_DIM = 128
ATTN_WIDTH = N_HEADS * HEAD_DIM
KV_WIDTH = N_KV_HEADS * HEAD_DIM
IDX_HEADS = 16
IDX_DIM = 64
TOPK_MAX = 256
ROPE_THETA = 500000.0
ROPE_FRACTION_DIV = 4
NORM_EPS = 1e-6

V7X_VMEM_BYTES = 64 * 1024 * 1024
VMEM_LIMIT_BYTES = V7X_VMEM_BYTES - 6 * 1024 * 1024

SUBLANES = 8

FFN_TM = 512
FFN_TF = 512
CONV_TM = 512
CONV_RC = 32
CONV_HALO = 32
PROJ_TM = 512
SEQ_CHUNK = 256
MERGE_TM = 512
MERGE_TN = 512

_NT = (((1,), (1,)), ((), ()))


def _cparams(sem):
    return pltpu.CompilerParams(dimension_semantics=sem, vmem_limit_bytes=VMEM_LIMIT_BYTES)


def _rms(x, gain):
    ms = jnp.mean(x * x, axis=-1, keepdims=True)
    return x * lax.rsqrt(ms + NORM_EPS) * gain


def _dot(a, b):
    return jnp.dot(a, b, preferred_element_type=F32)


def _ffn_kernel(*refs, nj, emit_h):
    if emit_h:
        x_ref, gpre_ref, w1g_ref, w1u_ref, w2_ref, gpost_ref, gnext_ref, o_ref, h_ref, hn_ref = refs
    else:
        x_ref, gpre_ref, w1g_ref, w1u_ref, w2_ref, gpost_ref, o_ref, hn_ref = refs
    j = pl.program_id(1)

    @pl.when(j == 0)
    def _():
        hn_ref[...] = _rms(x_ref[...], gpre_ref[...]).astype(BF16)

    hn = hn_ref[...]
    g = _dot(hn, w1g_ref[...])
    u = _dot(hn, w1u_ref[...])
    a = (g * jax.nn.sigmoid(g) * u).astype(BF16)
    contrib = _dot(a, w2_ref[...])

    @pl.when(j == 0)
    def _():
        o_ref[...] = contrib

    @pl.when(j > 0)
    def _():
        o_ref[...] += contrib

    @pl.when(j == nj - 1)
    def _():
        out = x_ref[...] + 0.5 * _rms(o_ref[...], gpost_ref[...])
        o_ref[...] = out
        if emit_h:
            h_ref[...] = _rms(out, gnext_ref[...]).astype(BF16)


def _ffn(x, gpre, w1, w2, gpost, gnext=None):
    n, d = x.shape
    f = w2.shape[0]
    tm, tf = FFN_TM, FFN_TF
    nj = f // tf
    assert n % tm == 0 and f % tf == 0
    emit_h = gnext is not None
    row = pl.BlockSpec((tm, d), lambda i, j: (i, 0))
    vec = pl.BlockSpec((1, d), lambda i, j: (0, 0))
    in_specs = [row, vec,
                pl.BlockSpec((d, tf), lambda i, j: (0, j)),
                pl.BlockSpec((d, tf), lambda i, j: (0, j + nj)),
                pl.BlockSpec((tf, d), lambda i, j: (j, 0)),
                vec]
    args = [x, gpre, w1, w1, w2, gpost]
    out_shape = [jax.ShapeDtypeStruct((n, d), F32)]
    out_specs = [row]
    if emit_h:
        in_specs.append(vec)
        args.append(gnext)
        out_shape.append(jax.ShapeDtypeStruct((n, d), BF16))
        out_specs.append(row)
    res = pl.pallas_call(
        functools.partial(_ffn_kernel, nj=nj, emit_h=emit_h),
        grid=(n // tm, nj),
        in_specs=in_specs,
        out_specs=out_specs,
        out_shape=out_shape,
        scratch_shapes=[pltpu.VMEM((tm, d), BF16)],
        compiler_params=_cparams(("parallel", "arbitrary")),
        name="ffn_emit_h" if emit_h else "ffn",
    )(*args)
    return res if emit_h else res[0]


def _conv_kernel(h_ref, wab_ref, dw_ref, dwb_ref, lng_ref, lnb_ref, c_ref, sh_ref, *, blocks_per_seq):
    tm = h_ref.shape[0]
    cw = c_ref.shape[1]
    halo = CONV_HALO
    i = pl.program_id(0)
    first = (i % blocks_per_seq) == 0

    @pl.when(first)
    def _():
        sh_ref[0, 0:halo, :] = jnp.zeros((halo, cw), F32)

    @pl.when(jnp.logical_not(first))
    def _():
        sh_ref[0, 0:halo, :] = sh_ref[0, tm:tm + halo, :]

    z = _dot(h_ref[...], wab_ref[...])
    sh_ref[0, halo:halo + tm, :] = z[:, :cw] * jax.nn.sigmoid(z[:, cw:])

    span = tm + halo - SUBLANES
    for s in range(1, SUBLANES):
        sh_ref[s, 0:span, :] = sh_ref[0, s:s + span, :]

    first_tap = halo - (CONV_KERNEL - 1)

    def chunk(rc, carry):
        r0 = pl.multiple_of(rc * CONV_RC, CONV_RC)
        acc = jnp.zeros((CONV_RC, cw), F32)
        for k in range(CONV_KERNEL):
            off = first_tap + k
            s, base = off % SUBLANES, off - off % SUBLANES
            acc = acc + dw_ref[k:k + 1, :] * sh_ref[s, pl.ds(r0 + base, CONV_RC), :]
        acc = acc + dwb_ref[...]
        mu = jnp.mean(acc, axis=-1, keepdims=True)
        xc = acc - mu
        var = jnp.mean(xc * xc, axis=-1, keepdims=True)
        y = xc * lax.rsqrt(var + NORM_EPS) * lng_ref[...] + lnb_ref[...]
        c_ref[pl.ds(r0, CONV_RC), :] = (y * jax.nn.sigmoid(y)).astype(BF16)
        return carry

    lax.fori_loop(0, tm // CONV_RC, chunk, 0)


def _conv_branch(h, wab, dw, dwb, lng, lnb, seq_len):
    n, d = h.shape
    cw = dw.shape[1]
    tm = CONV_TM
    assert seq_len % tm == 0 and CONV_HALO >= CONV_KERNEL - 1
    vec = pl.BlockSpec((1, cw), lambda i: (0, 0))
    return pl.pallas_call(
        functools.partial(_conv_kernel, blocks_per_seq=seq_len // tm),
        grid=(n // tm,),
        in_specs=[pl.BlockSpec((tm, d), lambda i: (i, 0)),
                  pl.BlockSpec((d, 2 * cw), lambda i: (0, 0)),
                  pl.BlockSpec((CONV_KERNEL, cw), lambda i: (0, 0)),
                  vec, vec, vec],
        out_specs=pl.BlockSpec((tm, cw), lambda i: (i, 0)),
        out_shape=jax.ShapeDtypeStruct((n, cw), BF16),
        scratch_shapes=[pltpu.VMEM((SUBLANES, tm + CONV_HALO, cw), F32)],
        compiler_params=_cparams(("arbitrary",)),
        name="conv_branch",
    )(h, wab, dw, dwb, lng, lnb)


_ROW_Q = 0
_ROW_K = _ROW_Q + ATTN_WIDTH
_ROW_V = _ROW_K + KV_WIDTH
_ROW_QI = _ROW_V + KV_WIDTH
_ROW_KI = _ROW_QI + IDX_HEADS * IDX_DIM
_ROW_WI = _ROW_KI + 128
_ROWS = _ROW_WI + IDX_HEADS


def _rope_rows(x, cos, sin, half):
    x1, x2 = x[0:half], x[half:2 * half]
    return jnp.concatenate([x1 * cos - x2 * sin, x2 * cos + x1 * sin, x[2 * half:]], axis=0)


def _proj_kernel(h_ref, pos_ref, wt_ref, fq_ref, fi_ref, qT_ref, qiT_ref, wiT_ref, kc_ref, vT_ref, kic_ref):
    tm = h_ref.shape[0]
    n_chunk = tm // SEQ_CHUNK
    h = h_ref[...]
    pos = pos_ref[...].astype(F32)
    ang_q = fq_ref[...] * pos
    cq, sq = jnp.cos(ang_q), jnp.sin(ang_q)
    ang_i = fi_ref[...] * pos
    ci, si = jnp.cos(ang_i), jnp.sin(ang_i)
    hq = HEAD_DIM // ROPE_FRACTION_DIV // 2
    hi = IDX_DIM // ROPE_FRACTION_DIV // 2

    def proj(lo, hi_):
        return lax.dot_general(wt_ref[lo:hi_, :], h, _NT, preferred_element_type=F32)

    zq = proj(_ROW_Q, _ROW_K)
    for hd in range(N_HEADS):
        b = hd * HEAD_DIM
        qT_ref[b:b + HEAD_DIM, :] = _rope_rows(zq[b:b + HEAD_DIM], cq, sq, hq).astype(BF16)

    zk = proj(_ROW_K, _ROW_V)
    kT = jnp.concatenate(
        [_rope_rows(zk[g * HEAD_DIM:(g + 1) * HEAD_DIM], cq, sq, hq) for g in range(N_KV_HEADS)], axis=0)
    kn = kT.T
    zv = proj(_ROW_V, _ROW_QI)
    for c in range(n_chunk):
        kc_ref[c] = kn[c * SEQ_CHUNK:(c + 1) * SEQ_CHUNK, :].astype(BF16)
        vT_ref[c] = zv[:, c * SEQ_CHUNK:(c + 1) * SEQ_CHUNK].astype(BF16)

    zqi = proj(_ROW_QI, _ROW_KI)
    for hd in range(IDX_HEADS):
        b = hd * IDX_DIM
        qiT_ref[b:b + IDX_DIM, :] = _rope_rows(zqi[b:b + IDX_DIM], ci, si, hi).astype(BF16)

    zr = proj(_ROW_KI, _ROWS)
    kin = _rope_rows(zr[0:128], ci, si, hi).T
    for c in range(n_chunk):
        kic_ref[c] = kin[c * SEQ_CHUNK:(c + 1) * SEQ_CHUNK, :].astype(BF16)
    wiT_ref[...] = zr[128:128 + IDX_HEADS] * (IDX_HEADS ** -0.5 * IDX_DIM ** -0.5)


def _projections(h, pos_row, wt, fq, fi):
    n, d = h.shape
    tm = PROJ_TM
    cpb = tm // SEQ_CHUNK
    n_chunks = n // SEQ_CHUNK
    col = lambda rows: pl.BlockSpec((rows, tm), lambda i: (0, i))
    return pl.pallas_call(
        _proj_kernel,
        grid=(n // tm,),
        in_specs=[pl.BlockSpec((tm, d), lambda i: (i, 0)),
                  pl.BlockSpec((1, tm), lambda i: (0, i)),
                  pl.BlockSpec((_ROWS, d), lambda i: (0, 0)),
                  pl.BlockSpec(fq.shape, lambda i: (0, 0)),
                  pl.BlockSpec(fi.shape, lambda i: (0, 0))],
        out_specs=[col(ATTN_WIDTH), col(IDX_HEADS * IDX_DIM), col(IDX_HEADS),
                   pl.BlockSpec((cpb, SEQ_CHUNK, KV_WIDTH), lambda i: (i, 0, 0)),
                   pl.BlockSpec((cpb, KV_WIDTH, SEQ_CHUNK), lambda i: (i, 0, 0)),
                   pl.BlockSpec((cpb, SEQ_CHUNK, 128), lambda i: (i, 0, 0))],
        out_shape=[jax.ShapeDtypeStruct((ATTN_WIDTH, n), BF16),
                   jax.ShapeDtypeStruct((IDX_HEADS * IDX_DIM, n), BF16),
                   jax.ShapeDtypeStruct((IDX_HEADS, n), F32),
                   jax.ShapeDtypeStruct((n_chunks, SEQ_CHUNK, KV_WIDTH), BF16),
                   jax.ShapeDtypeStruct((n_chunks, KV_WIDTH, SEQ_CHUNK), BF16),
                   jax.ShapeDtypeStruct((n_chunks, SEQ_CHUNK, 128), BF16)],
        compiler_params=_cparams(("parallel",)),
        name="qkv_indexer_proj",
    )(h, pos_row, wt, fq, fi)


def _fold(x):
    return x.reshape(x.shape[0] // SUBLANES, SUBLANES, x.shape[1])


def _dsa_kernel(qT_ref, qiT_ref, wiT_ref, kc_ref, vT_ref, kic_ref, o_ref, sc_ref, lg_ref, oacc_ref, *, topk):
    sc_len, tq = sc_ref.shape[1], sc_ref.shape[2]
    i = pl.program_id(1)
    n_vis = i + 1
    rep = N_HEADS // N_KV_HEADS
    neg_inf = jnp.float32(-jnp.inf)
    row = lax.broadcasted_iota(jnp.int32, (sc_len, tq), 0)
    lane = lax.broadcasted_iota(jnp.int32, (sc_len, tq), 1)
    diag_ok = row <= lane
    w = wiT_ref[...]

    def chunk_scores(c):
        kch = kic_ref[c][:, :IDX_DIM]
        acc = jnp.zeros((sc_len, tq), F32)
        for h in range(IDX_HEADS):
            d = _dot(kch, qiT_ref[h * IDX_DIM:(h + 1) * IDX_DIM, :])
            acc = acc + w[h:h + 1, :] * jnp.maximum(d, 0.0)
        return acc

    def score_body(c, carry):
        mx8, mn8 = carry
        s = chunk_scores(c)
        sc_ref[c] = s
        f = _fold(s)
        return jnp.maximum(mx8, f.max(axis=0)), jnp.minimum(mn8, f.min(axis=0))

    mx8, mn8 = lax.fori_loop(
        0, i, score_body,
        (jnp.full((SUBLANES, tq), -jnp.inf, F32), jnp.full((SUBLANES, tq), jnp.inf, F32)))
    s = chunk_scores(i)
    sc_ref[i] = jnp.where(diag_ok, s, neg_inf)
    mx8 = jnp.maximum(mx8, _fold(jnp.where(diag_ok, s, neg_inf)).max(axis=0))
    mn8 = jnp.minimum(mn8, _fold(jnp.where(diag_ok, s, jnp.inf)).min(axis=0))
    row_max = mx8.max(axis=0, keepdims=True)
    row_min = mn8.min(axis=0, keepdims=True)

    def count_ge(t):
        def body(c, cnt8):
            ind = jnp.where(sc_ref[c] >= t, 1.0, 0.0)
            return cnt8 + _fold(ind).sum(axis=0)
        cnt8 = lax.fori_loop(0, n_vis, body, jnp.zeros((SUBLANES, tq), F32))
        return cnt8.sum(axis=0, keepdims=True)

    n_valid = (i * tq + lane[0:1, :] + 1).astype(F32)
    kk = jnp.minimum(n_valid, float(topk))
    cnt_top = count_ge(row_max)
    top_is_enough = cnt_top >= kk
    lo = jnp.where(top_is_enough, row_max, row_min)
    cnt_lo = jnp.where(top_is_enough, cnt_top, n_valid)
    hi = row_max

    def active(lo, hi, cnt_lo):
        mid = 0.5 * lo + 0.5 * hi
        return jnp.where((cnt_lo != kk) & (mid > lo) & (mid < hi), 1, 0)

    def w_cond(carry):
        return carry[3] > 0

    def w_body(carry):
        lo, hi, cnt_lo, _ = carry
        mid = 0.5 * lo + 0.5 * hi
        act = active(lo, hi, cnt_lo) > 0
        cnt = count_ge(mid)
        up = act & (cnt >= kk)
        down = act & (cnt < kk)
        lo = jnp.where(up, mid, lo)
        cnt_lo = jnp.where(up, cnt, cnt_lo)
        hi = jnp.where(down, mid, hi)
        return lo, hi, cnt_lo, jnp.max(active(lo, hi, cnt_lo))

    lo, hi, cnt_lo, _ = lax.while_loop(w_cond, w_body, (lo, hi, cnt_lo, jnp.max(active(lo, hi, cnt_lo))))
    thr = lo

    c_exp = HEAD_DIM ** -0.5 * math.log2(math.e)
    for g in range(N_KV_HEADS):
        def logits_body(c, m8s):
            kch = kc_ref[c][:, g * HEAD_DIM:(g + 1) * HEAD_DIM]
            sel = sc_ref[c] >= thr
            out = []
            for r in range(rep):
                hd = g * rep + r
                l = _dot(kch, qT_ref[hd * HEAD_DIM:(hd + 1) * HEAD_DIM, :])
                l = jnp.where(sel, l, neg_inf)
                lg_ref[r, c] = l
                out.append(jnp.maximum(m8s[r], _fold(l).max(axis=0)))
            return tuple(out)

        m8s = lax.fori_loop(0, n_vis, logits_body,
                            tuple(jnp.full((SUBLANES, tq), -jnp.inf, F32) for _ in range(rep)))
        ms = [m8.max(axis=0, keepdims=True) for m8 in m8s]
        oacc_ref[...] = jnp.zeros(oacc_ref.shape, F32)

        def pv_body(c, s8s):
            vch = vT_ref[c][g * HEAD_DIM:(g + 1) * HEAD_DIM, :]
            out = []
            for r in range(rep):
                p = jnp.exp2((lg_ref[r, c] - ms[r]) * c_exp)
                oacc_ref[r] += _dot(vch, p.astype(BF16))
                out.append(s8s[r] + _fold(p).sum(axis=0))
            return tuple(out)

        s8s = lax.fori_loop(0, n_vis, pv_body, tuple(jnp.zeros((SUBLANES, tq), F32) for _ in range(rep)))
        for r in range(rep):
            hd = g * rep + r
            den = s8s[r].sum(axis=0, keepdims=True)
            o_ref[:, hd * HEAD_DIM:(hd + 1) * HEAD_DIM] = (oacc_ref[r] / den).T.astype(BF16)


def _sparse_attention(qT, qiT, wiT, kc, vT, kic, batch, seq_len):
    n = qT.shape[1]
    tq = SEQ_CHUNK
    n_q = seq_len // tq
    n_c = seq_len // SEQ_CHUNK
    topk = min(TOPK_MAX, seq_len // 4)
    qcol = lambda rows: pl.BlockSpec((rows, tq), lambda b, i: (0, b * n_q + i))
    seq = lambda r, c: pl.BlockSpec((n_c, r, c), lambda b, i: (b, 0, 0))
    return pl.pallas_call(
        functools.partial(_dsa_kernel, topk=topk),
        grid=(batch, n_q),
        in_specs=[qcol(ATTN_WIDTH), qcol(IDX_HEADS * IDX_DIM), qcol(IDX_HEADS),
                  seq(SEQ_CHUNK, KV_WIDTH), seq(KV_WIDTH, SEQ_CHUNK), seq(SEQ_CHUNK, 128)],
        out_specs=pl.BlockSpec((tq, ATTN_WIDTH), lambda b, i: (b * n_q + i, 0)),
        out_shape=jax.ShapeDtypeStruct((n, ATTN_WIDTH), BF16),
        scratch_shapes=[pltpu.VMEM((n_c, SEQ_CHUNK, tq), F32),
                        pltpu.VMEM((N_HEADS // N_KV_HEADS, n_c, SEQ_CHUNK, tq), F32),
                        pltpu.VMEM((N_HEADS // N_KV_HEADS, HEAD_DIM, tq), F32)],
        compiler_params=_cparams(("parallel", "arbitrary")),
        name="dsa_attention",
    )(qT, qiT, wiT, kc, vT, kic)


def _merge_kernel(x_ref, h_ref, c_ref, o_ref, wg0_ref, wg1_ref, wpw_ref, wo_ref, wout_ref, gpost_ref,
                  out_ref, merged_ref, y_ref, *, nj):
    tn = wg0_ref.shape[1]
    j = pl.program_id(1)

    @pl.when(j < nj)
    def _():
        h = h_ref[...]
        g0 = jax.nn.sigmoid(_dot(h, wg0_ref[...]))
        g1 = jax.nn.sigmoid(_dot(h, wg1_ref[...]))
        y_conv = _dot(c_ref[...], wpw_ref[...])
        y_attn = _dot(o_ref[...], wo_ref[...])
        merged_ref[j] = (g0 * y_conv + g1 * y_attn).astype(BF16)

    @pl.when(j >= nj)
    def _():
        acc = _dot(merged_ref[0], wout_ref[0:tn, :])
        for k in range(1, nj):
            acc = acc + _dot(merged_ref[k], wout_ref[k * tn:(k + 1) * tn, :])
        y_ref[j - nj] = acc

    @pl.when(j == 2 * nj - 1)
    def _():
        ss = jnp.zeros((y_ref.shape[1], 1), F32)
        for k in range(nj):
            yk = y_ref[k]
            ss = ss + jnp.sum(yk * yk, axis=-1, keepdims=True)
        scale = lax.rsqrt(ss / (nj * tn) + NORM_EPS)
        for k in range(nj):
            cols = slice(k * tn, (k + 1) * tn)
            out_ref[:, cols] = x_ref[:, cols] + y_ref[k] * scale * gpost_ref[:, cols]


def _merge(x, h, c, o, wg0, wg1, wpw, wo, wout, gpost):
    n, d = x.shape
    tm, tn = MERGE_TM, MERGE_TN
    nj = d // tn
    first = lambda i, j: (0, jnp.minimum(j, nj - 1))
    second = lambda i, j: (0, jnp.maximum(j - nj, 0))
    row = lambda width: pl.BlockSpec((tm, width), lambda i, j: (i, 0))
    return pl.pallas_call(
        functools.partial(_merge_kernel, nj=nj),
        grid=(n // tm, 2 * nj),
        in_specs=[row(d), row(d), row(c.shape[1]), row(o.shape[1]),
                  pl.BlockSpec((d, tn), first), pl.BlockSpec((d, tn), first),
                  pl.BlockSpec((wpw.shape[0], tn), first), pl.BlockSpec((wo.shape[0], tn), first),
                  pl.BlockSpec((d, tn), second),
                  pl.BlockSpec((1, d), lambda i, j: (0, 0))],
        out_specs=row(d),
        out_shape=jax.ShapeDtypeStruct((n, d), F32),
        scratch_shapes=[pltpu.VMEM((nj, tm, tn), BF16), pltpu.VMEM((nj, tm, tn), F32)],
        compiler_params=_cparams(("parallel", "arbitrary")),
        name="merge_out_proj",
    )(x, h, c, o, wg0, wg1, wpw, wo, wout, gpost)


def _inv_freq(rot_dims):
    half = rot_dims // 2
    return jnp.power(jnp.float32(ROPE_THETA), -jnp.arange(half, dtype=jnp.float32) * (2.0 / rot_dims))


def kernel(x, positions, ffn1_norm_pre, ffn1_w1, ffn1_w2, ffn1_norm_post, mix_norm_pre, w_in, conv_dw, conv_dw_b, conv_ln_g, conv_ln_b, conv_w_pw, attn_w_o, w_out, mix_norm_post, ffn2_norm_pre, ffn2_w1, ffn2_w2, ffn2_norm_post):
    batch, seq_len, d = x.shape
    depth = ffn1_w1.shape[0]
    n = batch * seq_len
    xf = x.reshape(n, d)
    pos_row = positions.reshape(1, n)
    fq = _inv_freq(HEAD_DIM // ROPE_FRACTION_DIV).reshape(-1, 1)
    fi = _inv_freq(IDX_DIM // ROPE_FRACTION_DIV).reshape(-1, 1)

    o_q = 2 * CONV_WIDTH
    o_k = o_q + ATTN_WIDTH
    o_v = o_k + KV_WIDTH
    o_qi = o_v + KV_WIDTH
    o_ki = o_qi + IDX_HEADS * IDX_DIM
    o_wi = o_ki + IDX_DIM
    o_g = o_wi + IDX_HEADS

    for l in range(depth):
        win = w_in[l]
        wab = win[:, :o_q].astype(BF16)
        wt = jnp.concatenate(
            [win[:, o_q:o_wi], jnp.zeros((d, 128 - IDX_DIM), win.dtype), win[:, o_wi:o_g]], axis=1).T.astype(BF16)
        wg0 = win[:, o_g:o_g + d].astype(BF16)
        wg1 = win[:, o_g + d:o_g + 2 * d].astype(BF16)

        xf, h = _ffn(xf, ffn1_norm_pre[l][None], ffn1_w1[l].astype(BF16), ffn1_w2[l].astype(BF16),
                     ffn1_norm_post[l][None], mix_norm_pre[l][None])
        c = _conv_branch(h, wab, conv_dw[l], conv_dw_b[l][None], conv_ln_g[l][None], conv_ln_b[l][None], seq_len)
        qT, qiT, wiT, kc, vT, kic = _projections(h, pos_row, wt, fq, fi)
        o = _sparse_attention(qT, qiT, wiT, kc, vT, kic, batch, seq_len)
        xf = _merge(xf, h, c, o, wg0, wg1, conv_w_pw[l].astype(BF16), attn_w_o[l].astype(BF16),
                    w_out[l].astype(BF16), mix_norm_post[l][None])
        xf = _ffn(xf, ffn2_norm_pre[l][None], ffn2_w1[l].astype(BF16), ffn2_w2[l].astype(BF16),
                  ffn2_norm_post[l][None])
    return xf.reshape(batch, seq_len, d)
```

```python
import functools
import math

import jax
import jax.numpy as jnp
from jax import lax
from jax.experimental import pallas as pl
from jax.experimental.pallas import tpu as pltpu

F32 = jnp.float32
BF16 = jnp.bfloat16

CONV_WIDTH = 1024
CONV_KERNEL = 31
N_HEADS = 8
N_KV_HEADS = 2
HEAD_DIM = 128
ATTN_WIDTH = N_HEADS * HEAD_DIM
KV_WIDTH = N_KV_HEADS * HEAD_DIM
IDX_HEADS = 16
IDX_DIM = 64
TOPK_MAX = 256
ROPE_THETA = 500000.0
ROPE_FRACTION_DIV = 4
NORM_EPS = 1e-6

V7X_VMEM_BYTES = 64 * 1024 * 1024
VMEM_LIMIT_BYTES = V7X_VMEM_BYTES - 6 * 1024 * 1024

SUBLANES = 8

FFN_TM = 512
FFN_TF = 512
CONV_TM = 512
CONV_RC = 32
CONV_HALO = 32
PROJ_TM = 512
SEQ_CHUNK = 512
Q_BLOCK = 256
BISECT_STEPS = 4
MERGE_TM = 512
MERGE_TN = 512

_NT = (((1,), (1,)), ((), ()))


def _cparams(sem):
    return pltpu.CompilerParams(dimension_semantics=sem, vmem_limit_bytes=VMEM_LIMIT_BYTES)


def _rms(x, gain):
    ms = jnp.mean(x * x, axis=-1, keepdims=True)
    return x * lax.rsqrt(ms + NORM_EPS) * gain


def _dot(a, b):
    return jnp.dot(a, b, preferred_element_type=F32)


def _ffn_kernel(*refs, nj, emit_h):
    if emit_h:
        x_ref, gpre_ref, w1g_ref, w1u_ref, w2_ref, gpost_ref, gnext_ref, o_ref, h_ref, hn_ref = refs
    else:
        x_ref, gpre_ref, w1g_ref, w1u_ref, w2_ref, gpost_ref, o_ref, hn_ref = refs
    j = pl.program_id(1)

    @pl.when(j == 0)
    def _():
        hn_ref[...] = _rms(x_ref[...], gpre_ref[...]).astype(BF16)
        o_ref[...] = jnp.zeros(o_ref.shape, F32)

    hn = hn_ref[...]
    g = _dot(hn, w1g_ref[...])
    u = _dot(hn, w1u_ref[...])
    a = (g * jax.nn.sigmoid(g) * u).astype(BF16)
    o_ref[...] += _dot(a, w2_ref[...])

    @pl.when(j == nj - 1)
    def _():
        out = x_ref[...] + 0.5 * _rms(o_ref[...], gpost_ref[...])
        o_ref[...] = out
        if emit_h:
            h_ref[...] = _rms(out, gnext_ref[...]).astype(BF16)


def _ffn(x, gpre, w1, w2, gpost, gnext=None):
    n, d = x.shape
    f = w2.shape[0]
    tm, tf = FFN_TM, FFN_TF
    nj = f // tf
    assert n % tm == 0 and f % tf == 0
    emit_h = gnext is not None
    row = pl.BlockSpec((tm, d), lambda i, j: (i, 0))
    vec = pl.BlockSpec((1, d), lambda i, j: (0, 0))
    in_specs = [row, vec,
                pl.BlockSpec((d, tf), lambda i, j: (0, j)),
                pl.BlockSpec((d, tf), lambda i, j: (0, j + nj)),
                pl.BlockSpec((tf, d), lambda i, j: (j, 0)),
                vec]
    args = [x, gpre, w1, w1, w2, gpost]
    out_shape = [jax.ShapeDtypeStruct((n, d), F32)]
    out_specs = [row]
    if emit_h:
        in_specs.append(vec)
        args.append(gnext)
        out_shape.append(jax.ShapeDtypeStruct((n, d), BF16))
        out_specs.append(row)
    res = pl.pallas_call(
        functools.partial(_ffn_kernel, nj=nj, emit_h=emit_h),
        grid=(n // tm, nj),
        in_specs=in_specs,
        out_specs=out_specs,
        out_shape=out_shape,
        scratch_shapes=[pltpu.VMEM((tm, d), BF16)],
        compiler_params=_cparams(("parallel", "arbitrary")),
        name="ffn_emit_h" if emit_h else "ffn",
    )(*args)
    return res if emit_h else res[0]


def _conv_kernel(h_ref, wab_ref, dw_ref, dwb_ref, lng_ref, lnb_ref, c_ref, sh_ref, *, blocks_per_seq):
    tm = h_ref.shape[0]
    cw = c_ref.shape[1]
    halo = CONV_HALO
    i = pl.program_id(0)
    first = (i % blocks_per_seq) == 0

    @pl.when(first)
    def _():
        sh_ref[0, 0:halo, :] = jnp.zeros((halo, cw), F32)

    @pl.when(jnp.logical_not(first))
    def _():
        sh_ref[0, 0:halo, :] = sh_ref[0, tm:tm + halo, :]

    z = _dot(h_ref[...], wab_ref[...])
    sh_ref[0, halo:halo + tm, :] = z[:, :cw] * jax.nn.sigmoid(z[:, cw:])

    span = tm + halo - SUBLANES
    for s in range(1, SUBLANES):
        sh_ref[s, 0:span, :] = sh_ref[0, s:s + span, :]

    first_tap = halo - (CONV_KERNEL - 1)

    def chunk(rc, carry):
        r0 = pl.multiple_of(rc * CONV_RC, CONV_RC)
        acc = jnp.zeros((CONV_RC, cw), F32)
        for k in range(CONV_KERNEL):
            off = first_tap + k
            s, base = off % SUBLANES, off - off % SUBLANES
            acc = acc + dw_ref[k:k + 1, :] * sh_ref[s, pl.ds(r0 + base, CONV_RC), :]
        acc = acc + dwb_ref[...]
        mu = jnp.mean(acc, axis=-1, keepdims=True)
        xc = acc - mu
        var = jnp.mean(xc * xc, axis=-1, keepdims=True)
        y = xc * lax.rsqrt(var + NORM_EPS) * lng_ref[...] + lnb_ref[...]
        c_ref[pl.ds(r0, CONV_RC), :] = (y * jax.nn.sigmoid(y)).astype(BF16)
        return carry

    lax.fori_loop(0, tm // CONV_RC, chunk, 0)


def _conv_branch(h, wab, dw, dwb, lng, lnb, seq_len):
    n, d = h.shape
    cw = dw.shape[1]
    tm = CONV_TM
    assert seq_len % tm == 0 and CONV_HALO >= CONV_KERNEL - 1
    vec = pl.BlockSpec((1, cw), lambda i: (0, 0))
    return pl.pallas_call(
        functools.partial(_conv_kernel, blocks_per_seq=seq_len // tm),
        grid=(n // tm,),
        in_specs=[pl.BlockSpec((tm, d), lambda i: (i, 0)),
                  pl.BlockSpec((d, 2 * cw), lambda i: (0, 0)),
                  pl.BlockSpec((CONV_KERNEL, cw), lambda i: (0, 0)),
                  vec, vec, vec],
        out_specs=pl.BlockSpec((tm, cw), lambda i: (i, 0)),
        out_shape=jax.ShapeDtypeStruct((n, cw), BF16),
        scratch_shapes=[pltpu.VMEM((SUBLANES, tm + CONV_HALO, cw), F32)],
        compiler_params=_cparams(("arbitrary",)),
        name="conv_branch",
    )(h, wab, dw, dwb, lng, lnb)


_ROW_Q = 0
_ROW_K = _ROW_Q + ATTN_WIDTH
_ROW_V = _ROW_K + KV_WIDTH
_ROW_QI = _ROW_V + KV_WIDTH
_ROW_KI = _ROW_QI + IDX_HEADS * IDX_DIM
_ROW_WI = _ROW_KI + 128
_ROWS = _ROW_WI + IDX_HEADS


def _rope_rows(x, cos, sin, half):
    x1, x2 = x[0:half], x[half:2 * half]
    return jnp.concatenate([x1 * cos - x2 * sin, x2 * cos + x1 * sin, x[2 * half:]], axis=0)


def _proj_kernel(h_ref, pos_ref, wt_ref, fq_ref, fi_ref, qT_ref, qiT_ref, wiT_ref, kc_ref, vT_ref, kic_ref):
    tm = h_ref.shape[0]
    n_chunk = tm // SEQ_CHUNK
    h = h_ref[...]
    pos = pos_ref[...].astype(F32)
    ang_q = fq_ref[...] * pos
    cq, sq = jnp.cos(ang_q), jnp.sin(ang_q)
    ang_i = fi_ref[...] * pos
    ci, si = jnp.cos(ang_i), jnp.sin(ang_i)
    hq = HEAD_DIM // ROPE_FRACTION_DIV // 2
    hi = IDX_DIM // ROPE_FRACTION_DIV // 2

    def proj(lo, hi_):
        return lax.dot_general(wt_ref[lo:hi_, :], h, _NT, preferred_element_type=F32)

    zq = proj(_ROW_Q, _ROW_K)
    for hd in range(N_HEADS):
        b = hd * HEAD_DIM
        qT_ref[b:b + HEAD_DIM, :] = _rope_rows(zq[b:b + HEAD_DIM], cq, sq, hq).astype(BF16)

    zk = proj(_ROW_K, _ROW_V)
    kT = jnp.concatenate(
        [_rope_rows(zk[g * HEAD_DIM:(g + 1) * HEAD_DIM], cq, sq, hq) for g in range(N_KV_HEADS)], axis=0)
    kn = kT.T
    zv = proj(_ROW_V, _ROW_QI)
    for c in range(n_chunk):
        kc_ref[c] = kn[c * SEQ_CHUNK:(c + 1) * SEQ_CHUNK, :].astype(BF16)
        vT_ref[c] = zv[:, c * SEQ_CHUNK:(c + 1) * SEQ_CHUNK].astype(BF16)

    zqi = proj(_ROW_QI, _ROW_KI)
    for hd in range(IDX_HEADS):
        b = hd * IDX_DIM
        qiT_ref[b:b + IDX_DIM, :] = _rope_rows(zqi[b:b + IDX_DIM], ci, si, hi).astype(BF16)

    zr = proj(_ROW_KI, _ROWS)
    kin = _rope_rows(zr[0:128], ci, si, hi).T
    for c in range(n_chunk):
        kic_ref[c] = kin[c * SEQ_CHUNK:(c + 1) * SEQ_CHUNK, :].astype(BF16)
    wiT_ref[...] = zr[128:128 + IDX_HEADS] * (IDX_HEADS ** -0.5 * IDX_DIM ** -0.5)


def _projections(h, pos_row, wt, fq, fi):
    n, d = h.shape
    tm = PROJ_TM
    cpb = tm // SEQ_CHUNK
    n_chunks = n // SEQ_CHUNK
    col = lambda rows: pl.BlockSpec((rows, tm), lambda i: (0, i))
    return pl.pallas_call(
        _proj_kernel,
        grid=(n // tm,),
        in_specs=[pl.BlockSpec((tm, d), lambda i: (i, 0)),
                  pl.BlockSpec((1, tm), lambda i: (0, i)),
                  pl.BlockSpec((_ROWS, d), lambda i: (0, 0)),
                  pl.BlockSpec(fq.shape, lambda i: (0, 0)),
                  pl.BlockSpec(fi.shape, lambda i: (0, 0))],
        out_specs=[col(ATTN_WIDTH), col(IDX_HEADS * IDX_DIM), col(IDX_HEADS),
                   pl.BlockSpec((cpb, SEQ_CHUNK, KV_WIDTH), lambda i: (i, 0, 0)),
                   pl.BlockSpec((cpb, KV_WIDTH, SEQ_CHUNK), lambda i: (i, 0, 0)),
                   pl.BlockSpec((cpb, SEQ_CHUNK, 128), lambda i: (i, 0, 0))],
        out_shape=[jax.ShapeDtypeStruct((ATTN_WIDTH, n), BF16),
                   jax.ShapeDtypeStruct((IDX_HEADS * IDX_DIM, n), BF16),
                   jax.ShapeDtypeStruct((IDX_HEADS, n), F32),
                   jax.ShapeDtypeStruct((n_chunks, SEQ_CHUNK, KV_WIDTH), BF16),
                   jax.ShapeDtypeStruct((n_chunks, KV_WIDTH, SEQ_CHUNK), BF16),
                   jax.ShapeDtypeStruct((n_chunks, SEQ_CHUNK, 128), BF16)],
        compiler_params=_cparams(("parallel",)),
        name="qkv_indexer_proj",
    )(h, pos_row, wt, fq, fi)


FOLD_ROWS = 4 * SUBLANES


def _fold(x):
    return x.reshape(x.shape[0] // FOLD_ROWS, FOLD_ROWS, x.shape[1])


def _dsa_kernel(qT_ref, qiT_ref, wiT_ref, kc_ref, vT_ref, kic_ref, o_ref, sc_ref, lg_ref, oacc_ref, *,
                topk, idx_steps):
    sc_len, tq = sc_ref.shape[1], sc_ref.shape[2]
    i = pl.program_id(1)
    n_vis = (i * tq) // sc_len + 1
    rep = N_HEADS // N_KV_HEADS
    neg_inf = jnp.float32(-jnp.inf)
    row = lax.broadcasted_iota(jnp.int32, (sc_len, tq), 0)
    lane = lax.broadcasted_iota(jnp.int32, (sc_len, tq), 1)
    rel = row - lane
    w = wiT_ref[...]

    def score_body(c, carry):
        mx8, mn8 = carry
        kch = kic_ref[c][:, :IDX_DIM]
        acc = jnp.zeros((sc_len, tq), F32)
        for h in range(IDX_HEADS):
            d = _dot(kch, qiT_ref[h * IDX_DIM:(h + 1) * IDX_DIM, :])
            acc = acc + w[h:h + 1, :] * jnp.maximum(d, 0.0)
        causal = rel <= i * tq - c * sc_len
        s = jnp.where(causal, acc, neg_inf)
        sc_ref[c] = s
        mx8 = jnp.maximum(mx8, _fold(s).max(axis=0))
        mn8 = jnp.minimum(mn8, _fold(jnp.where(causal, acc, jnp.inf)).min(axis=0))
        return mx8, mn8

    mx8, mn8 = lax.fori_loop(
        0, n_vis, score_body,
        (jnp.full((FOLD_ROWS, tq), -jnp.inf, F32), jnp.full((FOLD_ROWS, tq), jnp.inf, F32)))
    row_max = mx8.max(axis=0, keepdims=True)
    row_min = mn8.min(axis=0, keepdims=True)

    def count(pred):
        def body(c, cnt8):
            return cnt8 + _fold(jnp.where(pred(c, sc_ref[c]), 1.0, 0.0)).sum(axis=0)
        cnt8 = lax.fori_loop(0, n_vis, body, jnp.zeros((FOLD_ROWS, tq), F32))
        return cnt8.sum(axis=0, keepdims=True)

    def count_ge(t):
        return count(lambda c, s: s >= t)

    n_valid = (i * tq + lane[0:1, :] + 1).astype(F32)
    kk = jnp.minimum(n_valid, float(topk))
    cnt_top = count_ge(row_max)
    top_is_enough = cnt_top >= kk
    lo = jnp.where(top_is_enough, row_max, row_min)
    cnt_lo = jnp.where(top_is_enough, cnt_top, n_valid)
    hi = row_max
    cnt_hi = jnp.where(top_is_enough, 0.0, cnt_top)

    def active(lo, hi, cnt_lo):
        mid = 0.5 * lo + 0.5 * hi
        return (cnt_lo != kk) & (mid > lo) & (mid < hi)

    def any_lane(mask):
        return jnp.max(jnp.where(mask, 1, 0))

    def w_body(carry):
        lo, hi, cnt_lo, cnt_hi, _ = carry
        for _ in range(BISECT_STEPS):
            mid = 0.5 * lo + 0.5 * hi
            act = active(lo, hi, cnt_lo)
            cnt = count_ge(mid)
            up = act & (cnt >= kk)
            down = act & (cnt < kk)
            lo = jnp.where(up, mid, lo)
            cnt_lo = jnp.where(up, cnt, cnt_lo)
            hi = jnp.where(down, mid, hi)
            cnt_hi = jnp.where(down, cnt, cnt_hi)
        return lo, hi, cnt_lo, cnt_hi, any_lane(active(lo, hi, cnt_lo))

    lo, hi, cnt_lo, cnt_hi, _ = lax.while_loop(
        lambda carry: carry[4] > 0, w_body, (lo, hi, cnt_lo, cnt_hi, any_lane(active(lo, hi, cnt_lo))))
    thr = lo

    tie = cnt_lo > kk

    @pl.when(any_lane(tie) > 0)
    def _():
        need = kk - cnt_hi

        def key_index(c):
            return (c * sc_len + row).astype(F32)

        def j_body(_, carry):
            j_lo, j_hi = carry
            mid = jnp.floor(0.5 * (j_lo + j_hi))
            enough = count(lambda c, s: (s == thr) & (key_index(c) <= mid)) >= need
            return jnp.where(enough, j_lo, mid), jnp.where(enough, mid, j_hi)

        _, j_hi = lax.fori_loop(0, idx_steps, j_body, (jnp.full((1, tq), -1.0, F32), n_valid - 1.0))
        j_keep = jnp.where(tie, j_hi, jnp.inf)

        def drop_body(c, carry):
            s = sc_ref[c]
            sc_ref[c] = jnp.where((s == thr) & (key_index(c) > j_keep), neg_inf, s)
            return carry

        lax.fori_loop(0, n_vis, drop_body, 0)

    c_exp = HEAD_DIM ** -0.5 * math.log2(math.e)
    for g in range(N_KV_HEADS):
        def logits_body(c, m8s):
            kch = kc_ref[c][:, g * HEAD_DIM:(g + 1) * HEAD_DIM]
            sel = sc_ref[c] >= thr
            out = []
            for r in range(rep):
                hd = g * rep + r
                l = _dot(kch, qT_ref[hd * HEAD_DIM:(hd + 1) * HEAD_DIM, :])
                l = jnp.where(sel, l, neg_inf)
                lg_ref[r, c] = l
                out.append(jnp.maximum(m8s[r], _fold(l).max(axis=0)))
            return tuple(out)

        m8s = lax.fori_loop(0, n_vis, logits_body,
                            tuple(jnp.full((FOLD_ROWS, tq), -jnp.inf, F32) for _ in range(rep)))
        ms = [m8.max(axis=0, keepdims=True) for m8 in m8s]
        oacc_ref[...] = jnp.zeros(oacc_ref.shape, F32)

        def pv_body(c, s8s):
            vch = vT_ref[c][g * HEAD_DIM:(g + 1) * HEAD_DIM, :]
            out = []
            for r in range(rep):
                p = jnp.exp2((lg_ref[r, c] - ms[r]) * c_exp)
                oacc_ref[r] += _dot(vch, p.astype(BF16))
                out.append(s8s[r] + _fold(p).sum(axis=0))
            return tuple(out)

        s8s = lax.fori_loop(0, n_vis, pv_body, tuple(jnp.zeros((FOLD_ROWS, tq), F32) for _ in range(rep)))
        for r in range(rep):
            hd = g * rep + r
            den = s8s[r].sum(axis=0, keepdims=True)
            o_ref[:, hd * HEAD_DIM:(hd + 1) * HEAD_DIM] = (oacc_ref[r] / den).T.astype(BF16)


def _sparse_attention(qT, qiT, wiT, kc, vT, kic, batch, seq_len):
    n = qT.shape[1]
    tq = Q_BLOCK
    assert seq_len % SEQ_CHUNK == 0 and SEQ_CHUNK % tq == 0
    n_q = seq_len // tq
    n_c = seq_len // SEQ_CHUNK
    topk = min(TOPK_MAX, seq_len // 4)
    idx_steps = math.ceil(math.log2(seq_len)) + 1
    qcol = lambda rows: pl.BlockSpec((rows, tq), lambda b, i: (0, b * n_q + i))
    seq = lambda r, c: pl.BlockSpec((n_c, r, c), lambda b, i: (b, 0, 0))
    return pl.pallas_call(
        functools.partial(_dsa_kernel, topk=topk, idx_steps=idx_steps),
        grid=(batch, n_q),
        in_specs=[qcol(ATTN_WIDTH), qcol(IDX_HEADS * IDX_DIM), qcol(IDX_HEADS),
                  seq(SEQ_CHUNK, KV_WIDTH), seq(KV_WIDTH, SEQ_CHUNK), seq(SEQ_CHUNK, 128)],
        out_specs=pl.BlockSpec((tq, ATTN_WIDTH), lambda b, i: (b * n_q + i, 0)),
        out_shape=jax.ShapeDtypeStruct((n, ATTN_WIDTH), BF16),
        scratch_shapes=[pltpu.VMEM((n_c, SEQ_CHUNK, tq), F32),
                        pltpu.VMEM((N_HEADS // N_KV_HEADS, n_c, SEQ_CHUNK, tq), F32),
                        pltpu.VMEM((N_HEADS // N_KV_HEADS, HEAD_DIM, tq), F32)],
        compiler_params=_cparams(("parallel", "arbitrary")),
        name="dsa_attention",
    )(qT, qiT, wiT, kc, vT, kic)


def _merge_kernel(x_ref, h_ref, c_ref, o_ref, wg0_ref, wg1_ref, wpw_ref, wo_ref, wout_ref, gpost_ref,
                  out_ref, merged_ref, y_ref, *, nj):
    tn = wg0_ref.shape[1]
    j = pl.program_id(1)

    @pl.when(j < nj)
    def _():
        h = h_ref[...]
        g0 = jax.nn.sigmoid(_dot(h, wg0_ref[...]))
        g1 = jax.nn.sigmoid(_dot(h, wg1_ref[...]))
        y_conv = _dot(c_ref[...], wpw_ref[...])
        y_attn = _dot(o_ref[...], wo_ref[...])
        merged_ref[j] = (g0 * y_conv + g1 * y_attn).astype(BF16)

    @pl.when(j >= nj)
    def _():
        acc = _dot(merged_ref[0], wout_ref[0:tn, :])
        for k in range(1, nj):
            acc = acc + _dot(merged_ref[k], wout_ref[k * tn:(k + 1) * tn, :])
        y_ref[j - nj] = acc

    @pl.when(j == 2 * nj - 1)
    def _():
        ss = jnp.zeros((y_ref.shape[1], 1), F32)
        for k in range(nj):
            yk = y_ref[k]
            ss = ss + jnp.sum(yk * yk, axis=-1, keepdims=True)
        scale = lax.rsqrt(ss / (nj * tn) + NORM_EPS)
        for k in range(nj):
            cols = slice(k * tn, (k + 1) * tn)
            out_ref[:, cols] = x_ref[:, cols] + y_ref[k] * scale * gpost_ref[:, cols]


def _merge(x, h, c, o, wg0, wg1, wpw, wo, wout, gpost):
    n, d = x.shape
    tm, tn = MERGE_TM, MERGE_TN
    nj = d // tn
    first = lambda i, j: (0, jnp.minimum(j, nj - 1))
    second = lambda i, j: (0, jnp.maximum(j - nj, 0))
    row = lambda width: pl.BlockSpec((tm, width), lambda i, j: (i, 0))
    return pl.pallas_call(
        functools.partial(_merge_kernel, nj=nj),
        grid=(n // tm, 2 * nj),
        in_specs=[row(d), row(d), row(c.shape[1]), row(o.shape[1]),
                  pl.BlockSpec((d, tn), first), pl.BlockSpec((d, tn), first),
                  pl.BlockSpec((wpw.shape[0], tn), first), pl.BlockSpec((wo.shape[0], tn), first),
                  pl.BlockSpec((d, tn), second),
                  pl.BlockSpec((1, d), lambda i, j: (0, 0))],
        out_specs=row(d),
        out_shape=jax.ShapeDtypeStruct((n, d), F32),
        scratch_shapes=[pltpu.VMEM((nj, tm, tn), BF16), pltpu.VMEM((nj, tm, tn), F32)],
        compiler_params=_cparams(("parallel", "arbitrary")),
        name="merge_out_proj",
    )(x, h, c, o, wg0, wg1, wpw, wo, wout, gpost)


def _inv_freq(rot_dims):
    half = rot_dims // 2
    return jnp.power(jnp.float32(ROPE_THETA), -jnp.arange(half, dtype=jnp.float32) * (2.0 / rot_dims))


def kernel(x, positions, ffn1_norm_pre, ffn1_w1, ffn1_w2, ffn1_norm_post, mix_norm_pre, w_in, conv_dw, conv_dw_b, conv_ln_g, conv_ln_b, conv_w_pw, attn_w_o, w_out, mix_norm_post, ffn2_norm_pre, ffn2_w1, ffn2_w2, ffn2_norm_post):
    batch, seq_len, d = x.shape
    depth = ffn1_w1.shape[0]
    n = batch * seq_len
    xf = x.reshape(n, d)
    pos_row = positions.reshape(1, n)
    fq = _inv_freq(HEAD_DIM // ROPE_FRACTION_DIV).reshape(-1, 1)
    fi = _inv_freq(IDX_DIM // ROPE_FRACTION_DIV).reshape(-1, 1)

    o_q = 2 * CONV_WIDTH
    o_k = o_q + ATTN_WIDTH
    o_v = o_k + KV_WIDTH
    o_qi = o_v + KV_WIDTH
    o_ki = o_qi + IDX_HEADS * IDX_DIM
    o_wi = o_ki + IDX_DIM
    o_g = o_wi + IDX_HEADS

    for l in range(depth):
        win = w_in[l]
        wab = win[:, :o_q].astype(BF16)
        wt = jnp.concatenate(
            [win[:, o_q:o_wi], jnp.zeros((d, 128 - IDX_DIM), win.dtype), win[:, o_wi:o_g]], axis=1).T.astype(BF16)
        wg0 = win[:, o_g:o_g + d].astype(BF16)
        wg1 = win[:, o_g + d:o_g + 2 * d].astype(BF16)

        xf, h = _ffn(xf, ffn1_norm_pre[l][None], ffn1_w1[l].astype(BF16), ffn1_w2[l].astype(BF16),
                     ffn1_norm_post[l][None], mix_norm_pre[l][None])
        c = _conv_branch(h, wab, conv_dw[l], conv_dw_b[l][None], conv_ln_g[l][None], conv_ln_b[l][None], seq_len)
        qT, qiT, wiT, kc, vT, kic = _projections(h, pos_row, wt, fq, fi)
        o = _sparse_attention(qT, qiT, wiT, kc, vT, kic, batch, seq_len)
        xf = _merge(xf, h, c, o, wg0, wg1, conv_w_pw[l].astype(BF16), attn_w_o[l].astype(BF16),
                    w_out[l].astype(BF16), mix_norm_post[l][None])
        xf = _ffn(xf, ffn2_norm_pre[l][None], ffn2_w1[l].astype(BF16), ffn2_w2[l].astype(BF16),
                  ffn2_norm_post[l][None])
    return xf.reshape(batch, seq_len, d)
```

```python
import functools
import math

import jax
import jax.numpy as jnp
from jax import lax
from jax.experimental import pallas as pl
from jax.experimental.pallas import tpu as pltpu

F32 = jnp.float32
BF16 = jnp.bfloat16

CONV_WIDTH = 1024
CONV_KERNEL = 31
N_HEADS = 8
N_KV_HEADS = 2
HEAD_DIM = 128
ATTN_WIDTH = N_HEADS * HEAD_DIM
KV_WIDTH = N_KV_HEADS * HEAD_DIM
IDX_HEADS = 16
IDX_DIM = 64
TOPK_MAX = 256
ROPE_THETA = 500000.0
ROPE_FRACTION_DIV = 4
NORM_EPS = 1e-6

V7X_VMEM_BYTES = 64 * 1024 * 1024
VMEM_LIMIT_BYTES = V7X_VMEM_BYTES - 3 * 1024 * 1024

SUBLANES = 8

FFN_TM = 1024
FFN_TF = 256
CONV_TM = 512
CONV_RC = 32
CONV_HALO = 32
PROJ_TM = 512
SEQ_CHUNK = 512
Q_BLOCK = 256
BISECT_STEPS = 4
MERGE_TM = 1024
MERGE_TN = 512
OUT_TM = 512

_NT = (((1,), (1,)), ((), ()))


def _cparams(sem):
    return pltpu.CompilerParams(dimension_semantics=sem, vmem_limit_bytes=VMEM_LIMIT_BYTES)


def _rms(x, gain):
    ms = jnp.mean(x * x, axis=-1, keepdims=True)
    return x * lax.rsqrt(ms + NORM_EPS) * gain


def _dot(a, b):
    return jnp.dot(a, b, preferred_element_type=F32)


def _ffn_kernel(*refs, nj, emit_h):
    if emit_h:
        x_ref, gpre_ref, w1g_ref, w1u_ref, w2_ref, gpost_ref, gnext_ref, o_ref, h_ref, hn_ref = refs
    else:
        x_ref, gpre_ref, w1g_ref, w1u_ref, w2_ref, gpost_ref, o_ref, hn_ref = refs
    j = pl.program_id(1)

    @pl.when(j == 0)
    def _():
        hn_ref[...] = _rms(x_ref[...], gpre_ref[...]).astype(BF16)
        o_ref[...] = jnp.zeros(o_ref.shape, F32)

    hn = hn_ref[...]
    g = _dot(hn, w1g_ref[...])
    u = _dot(hn, w1u_ref[...])
    a = (g * jax.nn.sigmoid(g) * u).astype(BF16)
    o_ref[...] += _dot(a, w2_ref[...])

    @pl.when(j == nj - 1)
    def _():
        out = x_ref[...] + 0.5 * _rms(o_ref[...], gpost_ref[...])
        o_ref[...] = out
        if emit_h:
            h_ref[...] = _rms(out, gnext_ref[...]).astype(BF16)


def _ffn(x, gpre, w1, w2, gpost, gnext=None):
    n, d = x.shape
    f = w2.shape[0]
    tm, tf = FFN_TM, FFN_TF
    nj = f // tf
    assert n % tm == 0 and f % tf == 0
    emit_h = gnext is not None
    row = pl.BlockSpec((tm, d), lambda i, j: (i, 0))
    vec = pl.BlockSpec((1, d), lambda i, j: (0, 0))
    in_specs = [row, vec,
                pl.BlockSpec((d, tf), lambda i, j: (0, j)),
                pl.BlockSpec((d, tf), lambda i, j: (0, j + nj)),
                pl.BlockSpec((tf, d), lambda i, j: (j, 0)),
                vec]
    args = [x, gpre, w1, w1, w2, gpost]
    out_shape = [jax.ShapeDtypeStruct((n, d), F32)]
    out_specs = [row]
    if emit_h:
        in_specs.append(vec)
        args.append(gnext)
        out_shape.append(jax.ShapeDtypeStruct((n, d), BF16))
        out_specs.append(row)
    res = pl.pallas_call(
        functools.partial(_ffn_kernel, nj=nj, emit_h=emit_h),
        grid=(n // tm, nj),
        in_specs=in_specs,
        out_specs=out_specs,
        out_shape=out_shape,
        scratch_shapes=[pltpu.VMEM((tm, d), BF16)],
        compiler_params=_cparams(("parallel", "arbitrary")),
        name="ffn_emit_h" if emit_h else "ffn",
    )(*args)
    return res if emit_h else res[0]


def _conv_kernel(h_ref, wab_ref, dw_ref, dwb_ref, lng_ref, lnb_ref, c_ref, sh_ref, *, blocks_per_seq):
    tm = h_ref.shape[0]
    cw = c_ref.shape[1]
    halo = CONV_HALO
    i = pl.program_id(0)
    first = (i % blocks_per_seq) == 0

    @pl.when(first)
    def _():
        sh_ref[0, 0:halo, :] = jnp.zeros((halo, cw), F32)

    @pl.when(jnp.logical_not(first))
    def _():
        sh_ref[0, 0:halo, :] = sh_ref[0, tm:tm + halo, :]

    z = _dot(h_ref[...], wab_ref[...])
    sh_ref[0, halo:halo + tm, :] = z[:, :cw] * jax.nn.sigmoid(z[:, cw:])

    span = tm + halo - SUBLANES
    for s in range(1, SUBLANES):
        sh_ref[s, 0:span, :] = sh_ref[0, s:s + span, :]

    first_tap = halo - (CONV_KERNEL - 1)

    def chunk(rc, carry):
        r0 = pl.multiple_of(rc * CONV_RC, CONV_RC)
        acc = jnp.zeros((CONV_RC, cw), F32)
        for k in range(CONV_KERNEL):
            off = first_tap + k
            s, base = off % SUBLANES, off - off % SUBLANES
            acc = acc + dw_ref[k:k + 1, :] * sh_ref[s, pl.ds(r0 + base, CONV_RC), :]
        acc = acc + dwb_ref[...]
        mu = jnp.mean(acc, axis=-1, keepdims=True)
        xc = acc - mu
        var = jnp.mean(xc * xc, axis=-1, keepdims=True)
        y = xc * lax.rsqrt(var + NORM_EPS) * lng_ref[...] + lnb_ref[...]
        c_ref[pl.ds(r0, CONV_RC), :] = (y * jax.nn.sigmoid(y)).astype(BF16)
        return carry

    lax.fori_loop(0, tm // CONV_RC, chunk, 0)


def _conv_branch(h, wab, dw, dwb, lng, lnb, seq_len):
    n, d = h.shape
    cw = dw.shape[1]
    tm = CONV_TM
    assert seq_len % tm == 0 and CONV_HALO >= CONV_KERNEL - 1
    vec = pl.BlockSpec((1, cw), lambda i: (0, 0))
    return pl.pallas_call(
        functools.partial(_conv_kernel, blocks_per_seq=seq_len // tm),
        grid=(n // tm,),
        in_specs=[pl.BlockSpec((tm, d), lambda i: (i, 0)),
                  pl.BlockSpec((d, 2 * cw), lambda i: (0, 0)),
                  pl.BlockSpec((CONV_KERNEL, cw), lambda i: (0, 0)),
                  vec, vec, vec],
        out_specs=pl.BlockSpec((tm, cw), lambda i: (i, 0)),
        out_shape=jax.ShapeDtypeStruct((n, cw), BF16),
        scratch_shapes=[pltpu.VMEM((SUBLANES, tm + CONV_HALO, cw), F32)],
        compiler_params=_cparams(("arbitrary",)),
        name="conv_branch",
    )(h, wab, dw, dwb, lng, lnb)


_ROW_Q = 0
_ROW_K = _ROW_Q + ATTN_WIDTH
_ROW_V = _ROW_K + KV_WIDTH
_ROW_QI = _ROW_V + KV_WIDTH
_ROW_KI = _ROW_QI + IDX_HEADS * IDX_DIM
_ROW_WI = _ROW_KI + 128
_ROWS = _ROW_WI + IDX_HEADS


def _rope_rows(x, cos, sin, half):
    x1, x2 = x[0:half], x[half:2 * half]
    return jnp.concatenate([x1 * cos - x2 * sin, x2 * cos + x1 * sin, x[2 * half:]], axis=0)


def _proj_kernel(h_ref, pos_ref, wt_ref, fq_ref, fi_ref, qT_ref, qiT_ref, wiT_ref, kc_ref, vT_ref, kic_ref):
    tm = h_ref.shape[0]
    n_chunk = tm // SEQ_CHUNK
    h = h_ref[...]
    pos = pos_ref[...].astype(F32)
    ang_q = fq_ref[...] * pos
    cq, sq = jnp.cos(ang_q), jnp.sin(ang_q)
    ang_i = fi_ref[...] * pos
    ci, si = jnp.cos(ang_i), jnp.sin(ang_i)
    hq = HEAD_DIM // ROPE_FRACTION_DIV // 2
    hi = IDX_DIM // ROPE_FRACTION_DIV // 2

    def proj(lo, hi_):
        return lax.dot_general(wt_ref[lo:hi_, :], h, _NT, preferred_element_type=F32)

    zq = proj(_ROW_Q, _ROW_K)
    for hd in range(N_HEADS):
        b = hd * HEAD_DIM
        qT_ref[b:b + HEAD_DIM, :] = _rope_rows(zq[b:b + HEAD_DIM], cq, sq, hq).astype(BF16)

    zk = proj(_ROW_K, _ROW_V)
    kT = jnp.concatenate(
        [_rope_rows(zk[g * HEAD_DIM:(g + 1) * HEAD_DIM], cq, sq, hq) for g in range(N_KV_HEADS)], axis=0)
    kn = kT.T
    zv = proj(_ROW_V, _ROW_QI)
    for c in range(n_chunk):
        kc_ref[c] = kn[c * SEQ_CHUNK:(c + 1) * SEQ_CHUNK, :].astype(BF16)
        vT_ref[c] = zv[:, c * SEQ_CHUNK:(c + 1) * SEQ_CHUNK].astype(BF16)

    zqi = proj(_ROW_QI, _ROW_KI)
    for hd in range(IDX_HEADS):
        b = hd * IDX_DIM
        qiT_ref[b:b + IDX_DIM, :] = _rope_rows(zqi[b:b + IDX_DIM], ci, si, hi).astype(BF16)

    zr = proj(_ROW_KI, _ROWS)
    kin = _rope_rows(zr[0:128], ci, si, hi).T
    for c in range(n_chunk):
        kic_ref[c] = kin[c * SEQ_CHUNK:(c + 1) * SEQ_CHUNK, :].astype(BF16)
    wiT_ref[...] = zr[128:128 + IDX_HEADS] * (IDX_HEADS ** -0.5 * IDX_DIM ** -0.5)


def _projections(h, pos_row, wt, fq, fi):
    n, d = h.shape
    tm = PROJ_TM
    cpb = tm // SEQ_CHUNK
    n_chunks = n // SEQ_CHUNK
    col = lambda rows: pl.BlockSpec((rows, tm), lambda i: (0, i))
    return pl.pallas_call(
        _proj_kernel,
        grid=(n // tm,),
        in_specs=[pl.BlockSpec((tm, d), lambda i: (i, 0)),
                  pl.BlockSpec((1, tm), lambda i: (0, i)),
                  pl.BlockSpec((_ROWS, d), lambda i: (0, 0)),
                  pl.BlockSpec(fq.shape, lambda i: (0, 0)),
                  pl.BlockSpec(fi.shape, lambda i: (0, 0))],
        out_specs=[col(ATTN_WIDTH), col(IDX_HEADS * IDX_DIM), col(IDX_HEADS),
                   pl.BlockSpec((cpb, SEQ_CHUNK, KV_WIDTH), lambda i: (i, 0, 0)),
                   pl.BlockSpec((cpb, KV_WIDTH, SEQ_CHUNK), lambda i: (i, 0, 0)),
                   pl.BlockSpec((cpb, SEQ_CHUNK, 128), lambda i: (i, 0, 0))],
        out_shape=[jax.ShapeDtypeStruct((ATTN_WIDTH, n), BF16),
                   jax.ShapeDtypeStruct((IDX_HEADS * IDX_DIM, n), BF16),
                   jax.ShapeDtypeStruct((IDX_HEADS, n), F32),
                   jax.ShapeDtypeStruct((n_chunks, SEQ_CHUNK, KV_WIDTH), BF16),
                   jax.ShapeDtypeStruct((n_chunks, KV_WIDTH, SEQ_CHUNK), BF16),
                   jax.ShapeDtypeStruct((n_chunks, SEQ_CHUNK, 128), BF16)],
        compiler_params=_cparams(("parallel",)),
        name="qkv_indexer_proj",
    )(h, pos_row, wt, fq, fi)


FOLD_ROWS = 4 * SUBLANES


def _fold(x):
    return x.reshape(x.shape[0] // FOLD_ROWS, FOLD_ROWS, x.shape[1])


def _dsa_kernel(qT_ref, qiT_ref, wiT_ref, kc_ref, vT_ref, kic_ref, o_ref, sc_ref, lg_ref, oacc_ref, *,
                topk, idx_steps):
    sc_len, tq = sc_ref.shape[1], sc_ref.shape[2]
    i = pl.program_id(1)
    n_vis = (i * tq) // sc_len + 1
    rep = N_HEADS // N_KV_HEADS
    neg_inf = jnp.float32(-jnp.inf)
    row = lax.broadcasted_iota(jnp.int32, (sc_len, tq), 0)
    lane = lax.broadcasted_iota(jnp.int32, (sc_len, tq), 1)
    rel = row - lane
    w = wiT_ref[...]

    def score_body(c, carry):
        mx8, mn8 = carry
        kch = kic_ref[c][:, :IDX_DIM]
        acc = jnp.zeros((sc_len, tq), F32)
        for h in range(IDX_HEADS):
            d = _dot(kch, qiT_ref[h * IDX_DIM:(h + 1) * IDX_DIM, :])
            acc = acc + w[h:h + 1, :] * jnp.maximum(d, 0.0)
        causal = rel <= i * tq - c * sc_len
        s = jnp.where(causal, acc, neg_inf)
        sc_ref[c] = s
        mx8 = jnp.maximum(mx8, _fold(s).max(axis=0))
        mn8 = jnp.minimum(mn8, _fold(jnp.where(causal, acc, jnp.inf)).min(axis=0))
        return mx8, mn8

    mx8, mn8 = lax.fori_loop(
        0, n_vis, score_body,
        (jnp.full((FOLD_ROWS, tq), -jnp.inf, F32), jnp.full((FOLD_ROWS, tq), jnp.inf, F32)))
    row_max = mx8.max(axis=0, keepdims=True)
    row_min = mn8.min(axis=0, keepdims=True)

    def count(pred):
        def body(c, cnt8):
            return cnt8 + _fold(jnp.where(pred(c, sc_ref[c]), 1.0, 0.0)).sum(axis=0)
        cnt8 = lax.fori_loop(0, n_vis, body, jnp.zeros((FOLD_ROWS, tq), F32))
        return cnt8.sum(axis=0, keepdims=True)

    def count_ge(t):
        return count(lambda c, s: s >= t)

    n_valid = (i * tq + lane[0:1, :] + 1).astype(F32)
    kk = jnp.minimum(n_valid, float(topk))
    cnt_top = count_ge(row_max)
    top_is_enough = cnt_top >= kk
    lo = jnp.where(top_is_enough, row_max, row_min)
    cnt_lo = jnp.where(top_is_enough, cnt_top, n_valid)
    hi = row_max
    cnt_hi = jnp.where(top_is_enough, 0.0, cnt_top)

    def active(lo, hi, cnt_lo):
        mid = 0.5 * lo + 0.5 * hi
        return (cnt_lo != kk) & (mid > lo) & (mid < hi)

    def any_lane(mask):
        return jnp.max(jnp.where(mask, 1, 0))

    def w_body(carry):
        lo, hi, cnt_lo, cnt_hi, _ = carry
        for _ in range(BISECT_STEPS):
            mid = 0.5 * lo + 0.5 * hi
            act = active(lo, hi, cnt_lo)
            cnt = count_ge(mid)
            up = act & (cnt >= kk)
            down = act & (cnt < kk)
            lo = jnp.where(up, mid, lo)
            cnt_lo = jnp.where(up, cnt, cnt_lo)
            hi = jnp.where(down, mid, hi)
            cnt_hi = jnp.where(down, cnt, cnt_hi)
        return lo, hi, cnt_lo, cnt_hi, any_lane(active(lo, hi, cnt_lo))

    lo, hi, cnt_lo, cnt_hi, _ = lax.while_loop(
        lambda carry: carry[4] > 0, w_body, (lo, hi, cnt_lo, cnt_hi, any_lane(active(lo, hi, cnt_lo))))
    thr = lo

    tie = cnt_lo > kk

    @pl.when(any_lane(tie) > 0)
    def _():
        need = kk - cnt_hi

        def key_index(c):
            return (c * sc_len + row).astype(F32)

        def j_body(_, carry):
            j_lo, j_hi = carry
            mid = jnp.floor(0.5 * (j_lo + j_hi))
            enough = count(lambda c, s: (s == thr) & (key_index(c) <= mid)) >= need
            return jnp.where(enough, j_lo, mid), jnp.where(enough, mid, j_hi)

        _, j_hi = lax.fori_loop(0, idx_steps, j_body, (jnp.full((1, tq), -1.0, F32), n_valid - 1.0))
        j_keep = jnp.where(tie, j_hi, jnp.inf)

        def drop_body(c, carry):
            s = sc_ref[c]
            sc_ref[c] = jnp.where((s == thr) & (key_index(c) > j_keep), neg_inf, s)
            return carry

        lax.fori_loop(0, n_vis, drop_body, 0)

    c_exp = HEAD_DIM ** -0.5 * math.log2(math.e)
    for g in range(N_KV_HEADS):
        def logits_body(c, m8s):
            kch = kc_ref[c][:, g * HEAD_DIM:(g + 1) * HEAD_DIM]
            sel = sc_ref[c] >= thr
            out = []
            for r in range(rep):
                hd = g * rep + r
                l = _dot(kch, qT_ref[hd * HEAD_DIM:(hd + 1) * HEAD_DIM, :])
                l = jnp.where(sel, l, neg_inf)
                lg_ref[r, c] = l
                out.append(jnp.maximum(m8s[r], _fold(l).max(axis=0)))
            return tuple(out)

        m8s = lax.fori_loop(0, n_vis, logits_body,
                            tuple(jnp.full((FOLD_ROWS, tq), -jnp.inf, F32) for _ in range(rep)))
        ms = [m8.max(axis=0, keepdims=True) for m8 in m8s]
        oacc_ref[...] = jnp.zeros(oacc_ref.shape, F32)

        def pv_body(c, s8s):
            vch = vT_ref[c][g * HEAD_DIM:(g + 1) * HEAD_DIM, :]
            out = []
            for r in range(rep):
                p = jnp.exp2((lg_ref[r, c] - ms[r]) * c_exp)
                oacc_ref[r] += _dot(vch, p.astype(BF16))
                out.append(s8s[r] + _fold(p).sum(axis=0))
            return tuple(out)

        s8s = lax.fori_loop(0, n_vis, pv_body, tuple(jnp.zeros((FOLD_ROWS, tq), F32) for _ in range(rep)))
        for r in range(rep):
            hd = g * rep + r
            den = s8s[r].sum(axis=0, keepdims=True)
            o_ref[:, hd * HEAD_DIM:(hd + 1) * HEAD_DIM] = (oacc_ref[r] / den).T.astype(BF16)


def _sparse_attention(qT, qiT, wiT, kc, vT, kic, batch, seq_len):
    n = qT.shape[1]
    tq = Q_BLOCK
    assert seq_len % SEQ_CHUNK == 0 and SEQ_CHUNK % tq == 0
    n_q = seq_len // tq
    n_c = seq_len // SEQ_CHUNK
    topk = min(TOPK_MAX, seq_len // 4)
    idx_steps = math.ceil(math.log2(seq_len)) + 1
    qcol = lambda rows: pl.BlockSpec((rows, tq), lambda b, i: (0, b * n_q + i))
    seq = lambda r, c: pl.BlockSpec((n_c, r, c), lambda b, i: (b, 0, 0))
    return pl.pallas_call(
        functools.partial(_dsa_kernel, topk=topk, idx_steps=idx_steps),
        grid=(batch, n_q),
        in_specs=[qcol(ATTN_WIDTH), qcol(IDX_HEADS * IDX_DIM), qcol(IDX_HEADS),
                  seq(SEQ_CHUNK, KV_WIDTH), seq(KV_WIDTH, SEQ_CHUNK), seq(SEQ_CHUNK, 128)],
        out_specs=pl.BlockSpec((tq, ATTN_WIDTH), lambda b, i: (b * n_q + i, 0)),
        out_shape=jax.ShapeDtypeStruct((n, ATTN_WIDTH), BF16),
        scratch_shapes=[pltpu.VMEM((n_c, SEQ_CHUNK, tq), F32),
                        pltpu.VMEM((N_HEADS // N_KV_HEADS, n_c, SEQ_CHUNK, tq), F32),
                        pltpu.VMEM((N_HEADS // N_KV_HEADS, HEAD_DIM, tq), F32)],
        compiler_params=_cparams(("parallel", "arbitrary")),
        name="dsa_attention",
    )(qT, qiT, wiT, kc, vT, kic)


def _gate_merge_kernel(h_ref, c_ref, o_ref, wg0_ref, wg1_ref, wpw_ref, wo_ref, m_ref):
    h = h_ref[...]
    g0 = jax.nn.sigmoid(_dot(h, wg0_ref[...]))
    g1 = jax.nn.sigmoid(_dot(h, wg1_ref[...]))
    y_conv = _dot(c_ref[...], wpw_ref[...])
    y_attn = _dot(o_ref[...], wo_ref[...])
    m_ref[...] = (g0 * y_conv + g1 * y_attn).astype(BF16)


def _gate_merge(h, c, o, wg0, wg1, wpw, wo):
    n, d = h.shape
    tm, tn = MERGE_TM, MERGE_TN
    row = lambda width: pl.BlockSpec((tm, width), lambda j, i: (i, 0))
    colw = lambda rows: pl.BlockSpec((rows, tn), lambda j, i: (0, j))
    return pl.pallas_call(
        _gate_merge_kernel,
        grid=(d // tn, n // tm),
        in_specs=[row(d), row(c.shape[1]), row(o.shape[1]),
                  colw(d), colw(d), colw(wpw.shape[0]), colw(wo.shape[0])],
        out_specs=pl.BlockSpec((tm, tn), lambda j, i: (i, j)),
        out_shape=jax.ShapeDtypeStruct((n, d), BF16),
        compiler_params=_cparams(("parallel", "parallel")),
        name="gate_merge",
    )(h, c, o, wg0, wg1, wpw, wo)


def _out_proj_kernel(x_ref, m_ref, wout_ref, gpost_ref, out_ref):
    y = _dot(m_ref[...], wout_ref[...])
    out_ref[...] = x_ref[...] + _rms(y, gpost_ref[...])


def _out_proj(x, merged, wout, gpost):
    n, d = x.shape
    tm = OUT_TM
    row = pl.BlockSpec((tm, d), lambda i: (i, 0))
    return pl.pallas_call(
        _out_proj_kernel,
        grid=(n // tm,),
        in_specs=[row, row, pl.BlockSpec((d, d), lambda i: (0, 0)), pl.BlockSpec((1, d), lambda i: (0, 0))],
        out_specs=row,
        out_shape=jax.ShapeDtypeStruct((n, d), F32),
        compiler_params=_cparams(("parallel",)),
        name="out_proj",
    )(x, merged, wout, gpost)


def _inv_freq(rot_dims):
    half = rot_dims // 2
    return jnp.power(jnp.float32(ROPE_THETA), -jnp.arange(half, dtype=jnp.float32) * (2.0 / rot_dims))


def kernel(x, positions, ffn1_norm_pre, ffn1_w1, ffn1_w2, ffn1_norm_post, mix_norm_pre, w_in, conv_dw, conv_dw_b, conv_ln_g, conv_ln_b, conv_w_pw, attn_w_o, w_out, mix_norm_post, ffn2_norm_pre, ffn2_w1, ffn2_w2, ffn2_norm_post):
    batch, seq_len, d = x.shape
    depth = ffn1_w1.shape[0]
    n = batch * seq_len
    xf = x.reshape(n, d)
    pos_row = positions.reshape(1, n)
    fq = _inv_freq(HEAD_DIM // ROPE_FRACTION_DIV).reshape(-1, 1)
    fi = _inv_freq(IDX_DIM // ROPE_FRACTION_DIV).reshape(-1, 1)

    o_q = 2 * CONV_WIDTH
    o_k = o_q + ATTN_WIDTH
    o_v = o_k + KV_WIDTH
    o_qi = o_v + KV_WIDTH
    o_ki = o_qi + IDX_HEADS * IDX_DIM
    o_wi = o_ki + IDX_DIM
    o_g = o_wi + IDX_HEADS

    for l in range(depth):
        win = w_in[l]
        wab = win[:, :o_q].astype(BF16)
        wt = jnp.concatenate(
            [win[:, o_q:o_wi], jnp.zeros((d, 128 - IDX_DIM), win.dtype), win[:, o_wi:o_g]], axis=1).T.astype(BF16)
        wg0 = win[:, o_g:o_g + d].astype(BF16)
        wg1 = win[:, o_g + d:o_g + 2 * d].astype(BF16)

        xf, h = _ffn(xf, ffn1_norm_pre[l][None], ffn1_w1[l].astype(BF16), ffn1_w2[l].astype(BF16),
                     ffn1_norm_post[l][None], mix_norm_pre[l][None])
        c = _conv_branch(h, wab, conv_dw[l], conv_dw_b[l][None], conv_ln_g[l][None], conv_ln_b[l][None], seq_len)
        qT, qiT, wiT, kc, vT, kic = _projections(h, pos_row, wt, fq, fi)
        o = _sparse_attention(qT, qiT, wiT, kc, vT, kic, batch, seq_len)
        merged = _gate_merge(h, c, o, wg0, wg1, conv_w_pw[l].astype(BF16), attn_w_o[l].astype(BF16))
        xf = _out_proj(xf, merged, w_out[l].astype(BF16), mix_norm_post[l][None])
        xf = _ffn(xf, ffn2_norm_pre[l][None], ffn2_w1[l].astype(BF16), ffn2_w2[l].astype(BF16),
                  ffn2_norm_post[l][None])
    return xf.reshape(batch, seq_len, d)
```

```python
import functools
import math

import jax
import jax.numpy as jnp
from jax import lax
from jax.experimental import pallas as pl
from jax.experimental.pallas import tpu as pltpu

F32 = jnp.float32
BF16 = jnp.bfloat16

CONV_WIDTH = 1024
CONV_KERNEL = 31
N_HEADS = 8
N_KV_HEADS = 2
HEAD_DIM = 128
ATTN_WIDTH = N_HEADS * HEAD_DIM
KV_WIDTH = N_KV_HEADS * HEAD_DIM
IDX_HEADS = 16
IDX_DIM = 64
TOPK_MAX = 256
ROPE_THETA = 500000.0
ROPE_FRACTION_DIV = 4
NORM_EPS = 1e-6

V7X_VMEM_BYTES = 64 * 1024 * 1024
VMEM_LIMIT_BYTES = V7X_VMEM_BYTES - 3 * 1024 * 1024

SUBLANES = 8
LANES = 128

FFN_TM = 1024
FFN_TF = 256
CONV_TM = 512
CONV_RC = 32
CONV_LW = 512
CONV_HALO = 32
CONV_LN_ROWS = 128
PROJ_TM = 512
SEQ_CHUNK = 512
Q_BLOCK = 256
BISECT_STEPS = 4
MERGE_TM = 1024
MERGE_TN = 512
OUT_TM = 512

_NT = (((1,), (1,)), ((), ()))


def _cparams(sem):
    return pltpu.CompilerParams(dimension_semantics=sem, vmem_limit_bytes=VMEM_LIMIT_BYTES)


def _rms(x, gain):
    ms = jnp.mean(x * x, axis=-1, keepdims=True)
    return x * lax.rsqrt(ms + NORM_EPS) * gain


def _dot(a, b):
    return jnp.dot(a, b, preferred_element_type=F32)


def _ffn_kernel(*refs, nj, emit_h):
    if emit_h:
        x_ref, gpre_ref, w1g_ref, w1u_ref, w2_ref, gpost_ref, gnext_ref, o_ref, h_ref, hn_ref = refs
    else:
        x_ref, gpre_ref, w1g_ref, w1u_ref, w2_ref, gpost_ref, o_ref, hn_ref = refs
    j = pl.program_id(1)

    @pl.when(j == 0)
    def _():
        hn_ref[...] = _rms(x_ref[...], gpre_ref[...]).astype(BF16)
        o_ref[...] = jnp.zeros(o_ref.shape, F32)

    hn = hn_ref[...]
    g = _dot(hn, w1g_ref[...].astype(BF16))
    u = _dot(hn, w1u_ref[...].astype(BF16))
    a = (g * jax.nn.sigmoid(g) * u).astype(BF16)
    o_ref[...] += _dot(a, w2_ref[...].astype(BF16))

    @pl.when(j == nj - 1)
    def _():
        out = x_ref[...] + 0.5 * _rms(o_ref[...], gpost_ref[...])
        o_ref[...] = out
        if emit_h:
            h_ref[...] = _rms(out, gnext_ref[...]).astype(BF16)


def _ffn(x, gpre, w1, w2, gpost, gnext=None):
    n, d = x.shape
    f = w2.shape[0]
    tm, tf = FFN_TM, FFN_TF
    nj = f // tf
    assert n % tm == 0 and f % tf == 0
    emit_h = gnext is not None
    row = pl.BlockSpec((tm, d), lambda i, j: (i, 0))
    vec = pl.BlockSpec((1, d), lambda i, j: (0, 0))
    in_specs = [row, vec,
                pl.BlockSpec((d, tf), lambda i, j: (0, j)),
                pl.BlockSpec((d, tf), lambda i, j: (0, j + nj)),
                pl.BlockSpec((tf, d), lambda i, j: (j, 0)),
                vec]
    args = [x, gpre, w1, w1, w2, gpost]
    out_shape = [jax.ShapeDtypeStruct((n, d), F32)]
    out_specs = [row]
    if emit_h:
        in_specs.append(vec)
        args.append(gnext)
        out_shape.append(jax.ShapeDtypeStruct((n, d), BF16))
        out_specs.append(row)
    res = pl.pallas_call(
        functools.partial(_ffn_kernel, nj=nj, emit_h=emit_h),
        grid=(n // tm, nj),
        in_specs=in_specs,
        out_specs=out_specs,
        out_shape=out_shape,
        scratch_shapes=[pltpu.VMEM((tm, d), BF16)],
        compiler_params=_cparams(("parallel", "arbitrary")),
        name="ffn_emit_h" if emit_h else "ffn",
    )(*args)
    return res if emit_h else res[0]


def _conv_kernel(h_ref, wab_ref, dw_ref, dwb_ref, lng_ref, lnb_ref, c_ref, sh_ref, pre_ref, *, blocks_per_seq):
    tm = h_ref.shape[0]
    cw = c_ref.shape[1]
    halo = CONV_HALO
    i = pl.program_id(0)
    first = (i % blocks_per_seq) == 0

    @pl.when(first)
    def _():
        sh_ref[0, 0:halo, :] = jnp.zeros((halo, cw), F32)

    @pl.when(jnp.logical_not(first))
    def _():
        sh_ref[0, 0:halo, :] = sh_ref[0, tm:tm + halo, :]

    z = _dot(h_ref[...], wab_ref[...])
    sh_ref[0, halo:halo + tm, :] = z[:, :cw] * jax.nn.sigmoid(z[:, cw:])

    span = tm + halo - SUBLANES
    for s in range(1, SUBLANES):
        sh_ref[s, 0:span, :] = sh_ref[0, s:s + span, :]

    first_tap = halo - (CONV_KERNEL - 1)

    def taps(rc, carry):
        r0 = pl.multiple_of(rc * CONV_RC, CONV_RC)
        for lt in range(cw // CONV_LW):
            lanes = slice(lt * CONV_LW, (lt + 1) * CONV_LW)
            acc = jnp.zeros((CONV_RC // SUBLANES, SUBLANES, CONV_LW), F32)
            for k in range(CONV_KERNEL):
                off = first_tap + k
                s, base = off % SUBLANES, off - off % SUBLANES
                x = sh_ref[s, pl.ds(r0 + base, CONV_RC), lanes]
                acc = acc + dw_ref[k, :, lanes][None] * x.reshape(acc.shape)
            pre_ref[pl.ds(r0, CONV_RC), lanes] = acc.reshape(CONV_RC, CONV_LW)
        return carry

    lax.fori_loop(0, tm // CONV_RC, taps, 0)

    def norm(rb, carry):
        r0 = pl.multiple_of(rb * CONV_LN_ROWS, CONV_LN_ROWS)
        acc = pre_ref[pl.ds(r0, CONV_LN_ROWS), :] + dwb_ref[...]
        mu = jnp.mean(acc, axis=-1, keepdims=True)
        xc = acc - mu
        var = jnp.mean(xc * xc, axis=-1, keepdims=True)
        y = xc * lax.rsqrt(var + NORM_EPS) * lng_ref[...] + lnb_ref[...]
        c_ref[pl.ds(r0, CONV_LN_ROWS), :] = (y * jax.nn.sigmoid(y)).astype(BF16)
        return carry

    lax.fori_loop(0, tm // CONV_LN_ROWS, norm, 0)


def _conv_branch(h, wab, dw, dwb, lng, lnb, seq_len):
    n, d = h.shape
    cw = dw.shape[1]
    tm = CONV_TM
    assert seq_len % tm == 0 and CONV_HALO >= CONV_KERNEL - 1
    vec = pl.BlockSpec((1, cw), lambda i: (0, 0))
    dw_rows = jnp.broadcast_to(dw[:, None, :], (CONV_KERNEL, SUBLANES, cw))
    return pl.pallas_call(
        functools.partial(_conv_kernel, blocks_per_seq=seq_len // tm),
        grid=(n // tm,),
        in_specs=[pl.BlockSpec((tm, d), lambda i: (i, 0)),
                  pl.BlockSpec((d, 2 * cw), lambda i: (0, 0)),
                  pl.BlockSpec((CONV_KERNEL, SUBLANES, cw), lambda i: (0, 0, 0)),
                  vec, vec, vec],
        out_specs=pl.BlockSpec((tm, cw), lambda i: (i, 0)),
        out_shape=jax.ShapeDtypeStruct((n, cw), BF16),
        scratch_shapes=[pltpu.VMEM((SUBLANES, tm + CONV_HALO, cw), F32),
                        pltpu.VMEM((tm, cw), F32)],
        compiler_params=_cparams(("arbitrary",)),
        name="conv_branch",
    )(h, wab, dw_rows, dwb, lng, lnb)


_ROW_Q = 0
_ROW_K = _ROW_Q + ATTN_WIDTH
_ROW_V = _ROW_K + KV_WIDTH
_ROW_QI = _ROW_V + KV_WIDTH
_ROW_KI = _ROW_QI + IDX_HEADS * IDX_DIM
_ROW_WI = _ROW_KI + IDX_DIM
_ROWS = _ROW_WI + IDX_HEADS


def _rope_rows(x, cos, sin, half):
    x1, x2 = x[0:half], x[half:2 * half]
    return jnp.concatenate([x1 * cos - x2 * sin, x2 * cos + x1 * sin, x[2 * half:]], axis=0)


def _proj_kernel(h_ref, pos_ref, wt_ref, fq_ref, fi_ref, qT_ref, qiT_ref, wiT_ref, kc_ref, vT_ref, kic_ref):
    tm = h_ref.shape[0]
    n_chunk = tm // SEQ_CHUNK
    h = h_ref[...]
    pos = pos_ref[...].astype(F32)
    ang_q = fq_ref[...] * pos
    cq, sq = jnp.cos(ang_q), jnp.sin(ang_q)
    ang_i = fi_ref[...] * pos
    ci, si = jnp.cos(ang_i), jnp.sin(ang_i)
    hq = HEAD_DIM // ROPE_FRACTION_DIV // 2
    hi = IDX_DIM // ROPE_FRACTION_DIV // 2

    def proj(lo, hi_):
        return lax.dot_general(wt_ref[lo:hi_, :], h, _NT, preferred_element_type=F32)

    zq = proj(_ROW_Q, _ROW_K)
    for hd in range(N_HEADS):
        b = hd * HEAD_DIM
        qT_ref[b:b + HEAD_DIM, :] = _rope_rows(zq[b:b + HEAD_DIM], cq, sq, hq).astype(BF16)

    zk = proj(_ROW_K, _ROW_V)
    kT = jnp.concatenate(
        [_rope_rows(zk[g * HEAD_DIM:(g + 1) * HEAD_DIM], cq, sq, hq) for g in range(N_KV_HEADS)], axis=0)
    kn = kT.T
    zv = proj(_ROW_V, _ROW_QI)
    for c in range(n_chunk):
        kc_ref[c] = kn[c * SEQ_CHUNK:(c + 1) * SEQ_CHUNK, :].astype(BF16)
        vT_ref[c] = zv[:, c * SEQ_CHUNK:(c + 1) * SEQ_CHUNK].astype(BF16)

    zqi = proj(_ROW_QI, _ROW_KI)
    for hd in range(IDX_HEADS):
        b = hd * IDX_DIM
        qiT_ref[b:b + IDX_DIM, :] = _rope_rows(zqi[b:b + IDX_DIM], ci, si, hi).astype(BF16)

    zr = proj(_ROW_KI, _ROWS)
    kiT = jnp.concatenate([_rope_rows(zr[0:IDX_DIM], ci, si, hi), jnp.zeros((LANES - IDX_DIM, tm), F32)], axis=0)
    kin = kiT.T
    for c in range(n_chunk):
        kic_ref[c] = kin[c * SEQ_CHUNK:(c + 1) * SEQ_CHUNK, :].astype(BF16)
    wiT_ref[...] = zr[IDX_DIM:IDX_DIM + IDX_HEADS] * (IDX_HEADS ** -0.5 * IDX_DIM ** -0.5)


def _projections(h, pos_row, wt, fq, fi):
    n, d = h.shape
    tm = PROJ_TM
    cpb = tm // SEQ_CHUNK
    n_chunks = n // SEQ_CHUNK
    col = lambda rows: pl.BlockSpec((rows, tm), lambda i: (0, i))
    return pl.pallas_call(
        _proj_kernel,
        grid=(n // tm,),
        in_specs=[pl.BlockSpec((tm, d), lambda i: (i, 0)),
                  pl.BlockSpec((1, tm), lambda i: (0, i)),
                  pl.BlockSpec((_ROWS, d), lambda i: (0, 0)),
                  pl.BlockSpec(fq.shape, lambda i: (0, 0)),
                  pl.BlockSpec(fi.shape, lambda i: (0, 0))],
        out_specs=[col(ATTN_WIDTH), col(IDX_HEADS * IDX_DIM), col(IDX_HEADS),
                   pl.BlockSpec((cpb, SEQ_CHUNK, KV_WIDTH), lambda i: (i, 0, 0)),
                   pl.BlockSpec((cpb, KV_WIDTH, SEQ_CHUNK), lambda i: (i, 0, 0)),
                   pl.BlockSpec((cpb, SEQ_CHUNK, LANES), lambda i: (i, 0, 0))],
        out_shape=[jax.ShapeDtypeStruct((ATTN_WIDTH, n), BF16),
                   jax.ShapeDtypeStruct((IDX_HEADS * IDX_DIM, n), BF16),
                   jax.ShapeDtypeStruct((IDX_HEADS, n), F32),
                   jax.ShapeDtypeStruct((n_chunks, SEQ_CHUNK, KV_WIDTH), BF16),
                   jax.ShapeDtypeStruct((n_chunks, KV_WIDTH, SEQ_CHUNK), BF16),
                   jax.ShapeDtypeStruct((n_chunks, SEQ_CHUNK, LANES), BF16)],
        compiler_params=_cparams(("parallel",)),
        name="qkv_indexer_proj",
    )(h, pos_row, wt, fq, fi)


FOLD_ROWS = 4 * SUBLANES


def _fold(x):
    return x.reshape(x.shape[0] // FOLD_ROWS, FOLD_ROWS, x.shape[1])


def _dsa_kernel(qT_ref, qiT_ref, wiT_ref, kc_ref, vT_ref, kic_ref, o_ref, sc_ref, lg_ref, oacc_ref, *,
                topk, idx_steps):
    sc_len, tq = sc_ref.shape[1], sc_ref.shape[2]
    i = pl.program_id(1)
    n_vis = (i * tq) // sc_len + 1
    rep = N_HEADS // N_KV_HEADS
    neg_inf = jnp.float32(-jnp.inf)
    row = lax.broadcasted_iota(jnp.int32, (sc_len, tq), 0)
    lane = lax.broadcasted_iota(jnp.int32, (sc_len, tq), 1)
    rel = row - lane
    w = wiT_ref[...]

    def score_body(c, carry):
        mx8, mn8 = carry
        kch = kic_ref[c][:, :IDX_DIM]
        acc = jnp.zeros((sc_len, tq), F32)
        for h in range(IDX_HEADS):
            d = _dot(kch, qiT_ref[h * IDX_DIM:(h + 1) * IDX_DIM, :])
            acc = acc + w[h:h + 1, :] * jnp.maximum(d, 0.0)
        causal = rel <= i * tq - c * sc_len
        s = jnp.where(causal, acc, neg_inf)
        sc_ref[c] = s
        mx8 = jnp.maximum(mx8, _fold(s).max(axis=0))
        mn8 = jnp.minimum(mn8, _fold(jnp.where(causal, acc, jnp.inf)).min(axis=0))
        return mx8, mn8

    mx8, mn8 = lax.fori_loop(
        0, n_vis, score_body,
        (jnp.full((FOLD_ROWS, tq), -jnp.inf, F32), jnp.full((FOLD_ROWS, tq), jnp.inf, F32)))
    row_max = mx8.max(axis=0, keepdims=True)
    row_min = mn8.min(axis=0, keepdims=True)

    def count(pred):
        def body(c, cnt8):
            return cnt8 + _fold(jnp.where(pred(c, sc_ref[c]), 1.0, 0.0)).sum(axis=0)
        cnt8 = lax.fori_loop(0, n_vis, body, jnp.zeros((FOLD_ROWS, tq), F32))
        return cnt8.sum(axis=0, keepdims=True)

    def count_ge(t):
        return count(lambda c, s: s >= t)

    n_valid = (i * tq + lane[0:1, :] + 1).astype(F32)
    kk = jnp.minimum(n_valid, float(topk))
    cnt_top = count_ge(row_max)
    top_is_enough = cnt_top >= kk
    lo = jnp.where(top_is_enough, row_max, row_min)
    cnt_lo = jnp.where(top_is_enough, cnt_top, n_valid)
    hi = row_max
    cnt_hi = jnp.where(top_is_enough, 0.0, cnt_top)

    def active(lo, hi, cnt_lo):
        mid = 0.5 * lo + 0.5 * hi
        return (cnt_lo != kk) & (mid > lo) & (mid < hi)

    def any_lane(mask):
        return jnp.max(jnp.where(mask, 1, 0))

    def w_body(carry):
        lo, hi, cnt_lo, cnt_hi, _ = carry
        for _ in range(BISECT_STEPS):
            mid = 0.5 * lo + 0.5 * hi
            act = active(lo, hi, cnt_lo)
            cnt = count_ge(mid)
            up = act & (cnt >= kk)
            down = act & (cnt < kk)
            lo = jnp.where(up, mid, lo)
            cnt_lo = jnp.where(up, cnt, cnt_lo)
            hi = jnp.where(down, mid, hi)
            cnt_hi = jnp.where(down, cnt, cnt_hi)
        return lo, hi, cnt_lo, cnt_hi, any_lane(active(lo, hi, cnt_lo))

    lo, hi, cnt_lo, cnt_hi, _ = lax.while_loop(
        lambda carry: carry[4] > 0, w_body, (lo, hi, cnt_lo, cnt_hi, any_lane(active(lo, hi, cnt_lo))))
    thr = lo

    tie = cnt_lo > kk

    @pl.when(any_lane(tie) > 0)
    def _():
        need = kk - cnt_hi

        def key_index(c):
            return (c * sc_len + row).astype(F32)

        def j_body(_, carry):
            j_lo, j_hi = carry
            mid = jnp.floor(0.5 * (j_lo + j_hi))
            enough = count(lambda c, s: (s == thr) & (key_index(c) <= mid)) >= need
            return jnp.where(enough, j_lo, mid), jnp.where(enough, mid, j_hi)

        _, j_hi = lax.fori_loop(0, idx_steps, j_body, (jnp.full((1, tq), -1.0, F32), n_valid - 1.0))
        j_keep = jnp.where(tie, j_hi, jnp.inf)

        def drop_body(c, carry):
            s = sc_ref[c]
            sc_ref[c] = jnp.where((s == thr) & (key_index(c) > j_keep), neg_inf, s)
            return carry

        lax.fori_loop(0, n_vis, drop_body, 0)

    c_exp = HEAD_DIM ** -0.5 * math.log2(math.e)
    for g in range(N_KV_HEADS):
        def logits_body(c, m8s):
            kch = kc_ref[c][:, g * HEAD_DIM:(g + 1) * HEAD_DIM]
            sel = sc_ref[c] >= thr
            out = []
            for r in range(rep):
                hd = g * rep + r
                l = _dot(kch, qT_ref[hd * HEAD_DIM:(hd + 1) * HEAD_DIM, :])
                l = jnp.where(sel, l, neg_inf)
                lg_ref[r, c] = l
                out.append(jnp.maximum(m8s[r], _fold(l).max(axis=0)))
            return tuple(out)

        m8s = lax.fori_loop(0, n_vis, logits_body,
                            tuple(jnp.full((FOLD_ROWS, tq), -jnp.inf, F32) for _ in range(rep)))
        ms = [m8.max(axis=0, keepdims=True) for m8 in m8s]
        oacc_ref[...] = jnp.zeros(oacc_ref.shape, F32)

        def pv_body(c, s8s):
            vch = vT_ref[c][g * HEAD_DIM:(g + 1) * HEAD_DIM, :]
            out = []
            for r in range(rep):
                p = jnp.exp2((lg_ref[r, c] - ms[r]) * c_exp)
                oacc_ref[r] += _dot(vch, p.astype(BF16))
                out.append(s8s[r] + _fold(p).sum(axis=0))
            return tuple(out)

        s8s = lax.fori_loop(0, n_vis, pv_body, tuple(jnp.zeros((FOLD_ROWS, tq), F32) for _ in range(rep)))
        for r in range(rep):
            hd = g * rep + r
            den = s8s[r].sum(axis=0, keepdims=True)
            o_ref[:, hd * HEAD_DIM:(hd + 1) * HEAD_DIM] = (oacc_ref[r] / den).T.astype(BF16)


def _sparse_attention(qT, qiT, wiT, kc, vT, kic, batch, seq_len):
    n = qT.shape[1]
    tq = Q_BLOCK
    assert seq_len % SEQ_CHUNK == 0 and SEQ_CHUNK % tq == 0
    n_q = seq_len // tq
    n_c = seq_len // SEQ_CHUNK
    topk = min(TOPK_MAX, seq_len // 4)
    idx_steps = math.ceil(math.log2(seq_len)) + 1
    qcol = lambda rows: pl.BlockSpec((rows, tq), lambda b, i: (0, b * n_q + i))
    seq = lambda r, c: pl.BlockSpec((n_c, r, c), lambda b, i: (b, 0, 0))
    return pl.pallas_call(
        functools.partial(_dsa_kernel, topk=topk, idx_steps=idx_steps),
        grid=(batch, n_q),
        in_specs=[qcol(ATTN_WIDTH), qcol(IDX_HEADS * IDX_DIM), qcol(IDX_HEADS),
                  seq(SEQ_CHUNK, KV_WIDTH), seq(KV_WIDTH, SEQ_CHUNK), seq(SEQ_CHUNK, LANES)],
        out_specs=pl.BlockSpec((tq, ATTN_WIDTH), lambda b, i: (b * n_q + i, 0)),
        out_shape=jax.ShapeDtypeStruct((n, ATTN_WIDTH), BF16),
        scratch_shapes=[pltpu.VMEM((n_c, SEQ_CHUNK, tq), F32),
                        pltpu.VMEM((N_HEADS // N_KV_HEADS, n_c, SEQ_CHUNK, tq), F32),
                        pltpu.VMEM((N_HEADS // N_KV_HEADS, HEAD_DIM, tq), F32)],
        compiler_params=_cparams(("parallel", "arbitrary")),
        name="dsa_attention",
    )(qT, qiT, wiT, kc, vT, kic)


def _gate_merge_kernel(h_ref, c_ref, o_ref, wg0_ref, wg1_ref, wpw_ref, wo_ref, m_ref):
    h = h_ref[...]
    g0 = jax.nn.sigmoid(_dot(h, wg0_ref[...]))
    g1 = jax.nn.sigmoid(_dot(h, wg1_ref[...]))
    y_conv = _dot(c_ref[...], wpw_ref[...].astype(BF16))
    y_attn = _dot(o_ref[...], wo_ref[...].astype(BF16))
    m_ref[...] = (g0 * y_conv + g1 * y_attn).astype(BF16)


def _gate_merge(h, c, o, wg0, wg1, wpw, wo):
    n, d = h.shape
    tm, tn = MERGE_TM, MERGE_TN
    row = lambda width: pl.BlockSpec((tm, width), lambda j, i: (i, 0))
    colw = lambda rows: pl.BlockSpec((rows, tn), lambda j, i: (0, j))
    return pl.pallas_call(
        _gate_merge_kernel,
        grid=(d // tn, n // tm),
        in_specs=[row(d), row(c.shape[1]), row(o.shape[1]),
                  colw(d), colw(d), colw(wpw.shape[0]), colw(wo.shape[0])],
        out_specs=pl.BlockSpec((tm, tn), lambda j, i: (i, j)),
        out_shape=jax.ShapeDtypeStruct((n, d), BF16),
        compiler_params=_cparams(("parallel", "parallel")),
        name="gate_merge",
    )(h, c, o, wg0, wg1, wpw, wo)


def _out_proj_kernel(x_ref, m_ref, wout_ref, gpost_ref, out_ref):
    y = _dot(m_ref[...], wout_ref[...])
    out_ref[...] = x_ref[...] + _rms(y, gpost_ref[...])


def _out_proj(x, merged, wout, gpost):
    n, d = x.shape
    tm = OUT_TM
    row = pl.BlockSpec((tm, d), lambda i: (i, 0))
    return pl.pallas_call(
        _out_proj_kernel,
        grid=(n // tm,),
        in_specs=[row, row, pl.BlockSpec((d, d), lambda i: (0, 0)), pl.BlockSpec((1, d), lambda i: (0, 0))],
        out_specs=row,
        out_shape=jax.ShapeDtypeStruct((n, d), F32),
        compiler_params=_cparams(("parallel",)),
        name="out_proj",
    )(x, merged, wout, gpost)


def _inv_freq(rot_dims):
    half = rot_dims // 2
    return jnp.power(jnp.float32(ROPE_THETA), -jnp.arange(half, dtype=jnp.float32) * (2.0 / rot_dims))


def kernel(x, positions, ffn1_norm_pre, ffn1_w1, ffn1_w2, ffn1_norm_post, mix_norm_pre, w_in, conv_dw, conv_dw_b, conv_ln_g, conv_ln_b, conv_w_pw, attn_w_o, w_out, mix_norm_post, ffn2_norm_pre, ffn2_w1, ffn2_w2, ffn2_norm_post):
    batch, seq_len, d = x.shape
    depth = ffn1_w1.shape[0]
    n = batch * seq_len
    xf = x.reshape(n, d)
    pos_row = positions.reshape(1, n)
    fq = _inv_freq(HEAD_DIM // ROPE_FRACTION_DIV).reshape(-1, 1)
    fi = _inv_freq(IDX_DIM // ROPE_FRACTION_DIV).reshape(-1, 1)

    o_q = 2 * CONV_WIDTH
    o_k = o_q + ATTN_WIDTH
    o_v = o_k + KV_WIDTH
    o_qi = o_v + KV_WIDTH
    o_ki = o_qi + IDX_HEADS * IDX_DIM
    o_wi = o_ki + IDX_DIM
    o_g = o_wi + IDX_HEADS

    for l in range(depth):
        win = w_in[l]
        wab = win[:, :o_q].astype(BF16)
        wt = win[:, o_q:o_g].T.astype(BF16)
        wg0 = win[:, o_g:o_g + d].astype(BF16)
        wg1 = win[:, o_g + d:o_g + 2 * d].astype(BF16)

        xf, h = _ffn(xf, ffn1_norm_pre[l][None], ffn1_w1[l], ffn1_w2[l],
                     ffn1_norm_post[l][None], mix_norm_pre[l][None])
        c = _conv_branch(h, wab, conv_dw[l], conv_dw_b[l][None], conv_ln_g[l][None], conv_ln_b[l][None], seq_len)
        qT, qiT, wiT, kc, vT, kic = _projections(h, pos_row, wt, fq, fi)
        o = _sparse_attention(qT, qiT, wiT, kc, vT, kic, batch, seq_len)
        merged = _gate_merge(h, c, o, wg0, wg1, conv_w_pw[l], attn_w_o[l])
        xf = _out_proj(xf, merged, w_out[l].astype(BF16), mix_norm_post[l][None])
        xf = _ffn(xf, ffn2_norm_pre[l][None], ffn2_w1[l], ffn2_w2[l], ffn2_norm_post[l][None])
    return xf.reshape(batch, seq_len, d)
```

```python
import functools
import math

import jax
import jax.numpy as jnp
from jax import lax
from jax.experimental import pallas as pl
from jax.experimental.pallas import tpu as pltpu

F32 = jnp.float32
BF16 = jnp.bfloat16

CONV_WIDTH = 1024
CONV_KERNEL = 31
N_HEADS = 8
N_KV_HEADS = 2
HEAD_DIM = 128
ATTN_WIDTH = N_HEADS * HEAD_DIM
KV_WIDTH = N_KV_HEADS * HEAD_DIM
IDX_HEADS = 16
IDX_DIM = 64
TOPK_MAX = 256
ROPE_THETA = 500000.0
ROPE_FRACTION_DIV = 4
NORM_EPS = 1e-6

V7X_VMEM_BYTES = 64 * 1024 * 1024
VMEM_LIMIT_BYTES = V7X_VMEM_BYTES - 3 * 1024 * 1024

SUBLANES = 8
LANES = 128

FFN_TM = 1024
FFN_TF = 256
MIX_TM = 512
CONV_RC = 32
CONV_LW = 512
CONV_HALO = 32
CONV_LN_ROWS = 128
SEQ_CHUNK = 512
Q_BLOCK = 512
ATT_HEADS = 2
BISECT_STEPS = 4
MERGE_TM = 1024
MERGE_TN = 512
OUT_TM = 512

_NT = (((1,), (1,)), ((), ()))


def _cparams(sem):
    return pltpu.CompilerParams(dimension_semantics=sem, vmem_limit_bytes=VMEM_LIMIT_BYTES)


def _rms(x, gain):
    ms = jnp.mean(x * x, axis=-1, keepdims=True)
    return x * lax.rsqrt(ms + NORM_EPS) * gain


def _dot(a, b):
    return jnp.dot(a, b, preferred_element_type=F32)


def _ffn_kernel(*refs, nj, emit_h):
    if emit_h:
        x_ref, gpre_ref, w1g_ref, w1u_ref, w2_ref, gpost_ref, gnext_ref, o_ref, h_ref, hn_ref = refs
    else:
        x_ref, gpre_ref, w1g_ref, w1u_ref, w2_ref, gpost_ref, o_ref, hn_ref = refs
    j = pl.program_id(1)

    @pl.when(j == 0)
    def _():
        hn_ref[...] = _rms(x_ref[...], gpre_ref[...]).astype(BF16)
        o_ref[...] = jnp.zeros(o_ref.shape, F32)

    hn = hn_ref[...]
    g = _dot(hn, w1g_ref[...].astype(BF16))
    u = _dot(hn, w1u_ref[...].astype(BF16))
    a = (g * jax.nn.sigmoid(g) * u).astype(BF16)
    o_ref[...] += _dot(a, w2_ref[...].astype(BF16))

    @pl.when(j == nj - 1)
    def _():
        out = x_ref[...] + 0.5 * _rms(o_ref[...], gpost_ref[...])
        o_ref[...] = out
        if emit_h:
            h_ref[...] = _rms(out, gnext_ref[...]).astype(BF16)


def _ffn(x, gpre, w1, w2, gpost, gnext=None):
    n, d = x.shape
    f = w2.shape[0]
    tm, tf = FFN_TM, FFN_TF
    nj = f // tf
    assert n % tm == 0 and f % tf == 0
    emit_h = gnext is not None
    row = pl.BlockSpec((tm, d), lambda i, j: (i, 0))
    vec = pl.BlockSpec((1, d), lambda i, j: (0, 0))
    in_specs = [row, vec,
                pl.BlockSpec((d, tf), lambda i, j: (0, j)),
                pl.BlockSpec((d, tf), lambda i, j: (0, j + nj)),
                pl.BlockSpec((tf, d), lambda i, j: (j, 0)),
                vec]
    args = [x, gpre, w1, w1, w2, gpost]
    out_shape = [jax.ShapeDtypeStruct((n, d), F32)]
    out_specs = [row]
    if emit_h:
        in_specs.append(vec)
        args.append(gnext)
        out_shape.append(jax.ShapeDtypeStruct((n, d), BF16))
        out_specs.append(row)
    res = pl.pallas_call(
        functools.partial(_ffn_kernel, nj=nj, emit_h=emit_h),
        grid=(n // tm, nj),
        in_specs=in_specs,
        out_specs=out_specs,
        out_shape=out_shape,
        scratch_shapes=[pltpu.VMEM((tm, d), BF16)],
        compiler_params=_cparams(("parallel", "arbitrary")),
        name="ffn_emit_h" if emit_h else "ffn",
    )(*args)
    return res if emit_h else res[0]


def _mixer_in_kernel(h_ref, pos_ref, wab_ref, wt_ref, fq_ref, fi_ref, dw_ref, dwb_ref, lng_ref, lnb_ref,
                     c_ref, qT_ref, qiT_ref, wiT_ref, kc_ref, vT_ref, kic_ref, sh_ref, pre_ref, *, blocks_per_seq):
    tm = h_ref.shape[0]
    cw = c_ref.shape[1]
    halo = CONV_HALO
    i = pl.program_id(0)
    first = (i % blocks_per_seq) == 0

    @pl.when(first)
    def _():
        sh_ref[0, 0:halo, :] = jnp.zeros((halo, cw), F32)

    @pl.when(jnp.logical_not(first))
    def _():
        sh_ref[0, 0:halo, :] = sh_ref[0, tm:tm + halo, :]

    z = _dot(h_ref[...], wab_ref[...])
    sh_ref[0, halo:halo + tm, :] = z[:, :cw] * jax.nn.sigmoid(z[:, cw:])

    span = tm + halo - SUBLANES
    for s in range(1, SUBLANES):
        sh_ref[s, 0:span, :] = sh_ref[0, s:s + span, :]

    first_tap = halo - (CONV_KERNEL - 1)

    def taps(r0):
        for lt in range(cw // CONV_LW):
            lanes = slice(lt * CONV_LW, (lt + 1) * CONV_LW)
            acc = jnp.zeros((CONV_RC // SUBLANES, SUBLANES, CONV_LW), F32)
            for k in range(CONV_KERNEL):
                off = first_tap + k
                s, base = off % SUBLANES, off - off % SUBLANES
                x = sh_ref[s, r0 + base:r0 + base + CONV_RC, lanes]
                acc = acc + dw_ref[k, :, lanes][None] * x.reshape(acc.shape)
            pre_ref[r0:r0 + CONV_RC, lanes] = acc.reshape(CONV_RC, CONV_LW)

    stages = _proj_stages(h_ref, pos_ref, wt_ref, fq_ref, fi_ref, qT_ref, qiT_ref, wiT_ref, kc_ref, vT_ref, kic_ref)
    chunks = list(range(0, tm, CONV_RC))
    share = len(chunks) // len(stages)
    for si, stage in enumerate(stages):
        stage()
        last = len(chunks) if si == len(stages) - 1 else (si + 1) * share
        for r0 in chunks[si * share:last]:
            taps(r0)

    def norm(rb, carry):
        r0 = pl.multiple_of(rb * CONV_LN_ROWS, CONV_LN_ROWS)
        acc = pre_ref[pl.ds(r0, CONV_LN_ROWS), :] + dwb_ref[...]
        mu = jnp.mean(acc, axis=-1, keepdims=True)
        xc = acc - mu
        var = jnp.mean(xc * xc, axis=-1, keepdims=True)
        y = xc * lax.rsqrt(var + NORM_EPS) * lng_ref[...] + lnb_ref[...]
        c_ref[pl.ds(r0, CONV_LN_ROWS), :] = (y * jax.nn.sigmoid(y)).astype(BF16)
        return carry

    lax.fori_loop(0, tm // CONV_LN_ROWS, norm, 0)


def _mixer_inputs(h, pos_row, wab, wt, fq, fi, dw, dwb, lng, lnb, seq_len):
    n, d = h.shape
    cw = dw.shape[1]
    tm = MIX_TM
    assert seq_len % tm == 0 and tm % SEQ_CHUNK == 0 and CONV_HALO >= CONV_KERNEL - 1
    cpb = tm // SEQ_CHUNK
    n_chunks = n // SEQ_CHUNK
    vec = pl.BlockSpec((1, cw), lambda i: (0, 0))
    col = lambda rows: pl.BlockSpec((rows, tm), lambda i: (0, i))
    const = lambda shape: pl.BlockSpec(shape, lambda i: (0, 0), pipeline_mode=pl.Buffered(1))
    dw_rows = jnp.broadcast_to(dw[:, None, :], (CONV_KERNEL, SUBLANES, cw))
    return pl.pallas_call(
        functools.partial(_mixer_in_kernel, blocks_per_seq=seq_len // tm),
        grid=(n // tm,),
        in_specs=[pl.BlockSpec((tm, d), lambda i: (i, 0)),
                  pl.BlockSpec((1, tm), lambda i: (0, i)),
                  const((d, 2 * cw)), const((_ROWS, d)),
                  pl.BlockSpec(fq.shape, lambda i: (0, 0)),
                  pl.BlockSpec(fi.shape, lambda i: (0, 0)),
                  pl.BlockSpec((CONV_KERNEL, SUBLANES, cw), lambda i: (0, 0, 0)),
                  vec, vec, vec],
        out_specs=[pl.BlockSpec((tm, cw), lambda i: (i, 0)),
                   col(ATTN_WIDTH), col(IDX_HEADS * IDX_DIM), col(IDX_HEADS),
                   pl.BlockSpec((cpb, SEQ_CHUNK, KV_WIDTH), lambda i: (i, 0, 0)),
                   pl.BlockSpec((cpb, KV_WIDTH, SEQ_CHUNK), lambda i: (i, 0, 0)),
                   pl.BlockSpec((cpb, SEQ_CHUNK, LANES), lambda i: (i, 0, 0))],
        out_shape=[jax.ShapeDtypeStruct((n, cw), BF16),
                   jax.ShapeDtypeStruct((ATTN_WIDTH, n), BF16),
                   jax.ShapeDtypeStruct((IDX_HEADS * IDX_DIM, n), BF16),
                   jax.ShapeDtypeStruct((IDX_HEADS, n), F32),
                   jax.ShapeDtypeStruct((n_chunks, SEQ_CHUNK, KV_WIDTH), BF16),
                   jax.ShapeDtypeStruct((n_chunks, KV_WIDTH, SEQ_CHUNK), BF16),
                   jax.ShapeDtypeStruct((n_chunks, SEQ_CHUNK, LANES), BF16)],
        scratch_shapes=[pltpu.VMEM((SUBLANES, tm + CONV_HALO, cw), F32),
                        pltpu.VMEM((tm, cw), F32)],
        compiler_params=_cparams(("arbitrary",)),
        name="mixer_inputs",
    )(h, pos_row, wab, wt, fq, fi, dw_rows, dwb, lng, lnb)


_ROW_Q = 0
_ROW_K = _ROW_Q + ATTN_WIDTH
_ROW_V = _ROW_K + KV_WIDTH
_ROW_QI = _ROW_V + KV_WIDTH
_ROW_KI = _ROW_QI + IDX_HEADS * IDX_DIM
_ROW_WI = _ROW_KI + IDX_DIM
_ROWS = _ROW_WI + IDX_HEADS


def _rope_rows(x, cos, sin, half):
    x1, x2 = x[0:half], x[half:2 * half]
    return jnp.concatenate([x1 * cos - x2 * sin, x2 * cos + x1 * sin, x[2 * half:]], axis=0)


def _proj_stages(h_ref, pos_ref, wt_ref, fq_ref, fi_ref, qT_ref, qiT_ref, wiT_ref, kc_ref, vT_ref, kic_ref):
    tm = h_ref.shape[0]
    n_chunk = tm // SEQ_CHUNK
    hq = HEAD_DIM // ROPE_FRACTION_DIV // 2
    hi = IDX_DIM // ROPE_FRACTION_DIV // 2

    def rot(f_ref):
        ang = f_ref[...] * pos_ref[...].astype(F32)
        return jnp.cos(ang), jnp.sin(ang)

    def proj(lo, hi_):
        return lax.dot_general(wt_ref[lo:hi_, :], h_ref[...], _NT, preferred_element_type=F32)

    def queries():
        cq, sq = rot(fq_ref)
        zq = proj(_ROW_Q, _ROW_K)
        for hd in range(N_HEADS):
            b = hd * HEAD_DIM
            qT_ref[b:b + HEAD_DIM, :] = _rope_rows(zq[b:b + HEAD_DIM], cq, sq, hq).astype(BF16)

    def keys_values():
        cq, sq = rot(fq_ref)
        zk = proj(_ROW_K, _ROW_V)
        kT = jnp.concatenate(
            [_rope_rows(zk[g * HEAD_DIM:(g + 1) * HEAD_DIM], cq, sq, hq) for g in range(N_KV_HEADS)], axis=0)
        kn = kT.T
        zv = proj(_ROW_V, _ROW_QI)
        for c in range(n_chunk):
            kc_ref[c] = kn[c * SEQ_CHUNK:(c + 1) * SEQ_CHUNK, :].astype(BF16)
            vT_ref[c] = zv[:, c * SEQ_CHUNK:(c + 1) * SEQ_CHUNK].astype(BF16)

    def indexer_queries():
        ci, si = rot(fi_ref)
        zqi = proj(_ROW_QI, _ROW_KI)
        for hd in range(IDX_HEADS):
            b = hd * IDX_DIM
            qiT_ref[b:b + IDX_DIM, :] = _rope_rows(zqi[b:b + IDX_DIM], ci, si, hi).astype(BF16)

    def indexer_keys():
        ci, si = rot(fi_ref)
        zr = proj(_ROW_KI, _ROWS)
        kiT = jnp.concatenate(
            [_rope_rows(zr[0:IDX_DIM], ci, si, hi), jnp.zeros((LANES - IDX_DIM, tm), F32)], axis=0)
        kin = kiT.T
        for c in range(n_chunk):
            kic_ref[c] = kin[c * SEQ_CHUNK:(c + 1) * SEQ_CHUNK, :].astype(BF16)
        wiT_ref[...] = zr[IDX_DIM:IDX_DIM + IDX_HEADS] * (IDX_HEADS ** -0.5 * IDX_DIM ** -0.5)

    return [queries, keys_values, indexer_queries, indexer_keys]


FOLD_ROWS = 4 * SUBLANES


def _fold(x):
    return x.reshape(x.shape[0] // FOLD_ROWS, FOLD_ROWS, x.shape[1])


def _dsa_kernel(qT_ref, qiT_ref, wiT_ref, kc_ref, vT_ref, kic_ref, o_ref, sc_ref, lg_ref, oacc_ref, *,
                topk, idx_steps):
    sc_len, tq = sc_ref.shape[1], sc_ref.shape[2]
    i = pl.program_id(1)
    n_vis = (i * tq) // sc_len + 1
    rep = N_HEADS // N_KV_HEADS
    neg_inf = jnp.float32(-jnp.inf)
    row = lax.broadcasted_iota(jnp.int32, (sc_len, tq), 0)
    lane = lax.broadcasted_iota(jnp.int32, (sc_len, tq), 1)
    rel = row - lane
    w = wiT_ref[...]

    def score_body(c, carry):
        mx8, mn8 = carry
        kch = kic_ref[c][:, :IDX_DIM]
        acc = jnp.zeros((sc_len, tq), F32)
        for h in range(IDX_HEADS):
            d = _dot(kch, qiT_ref[h * IDX_DIM:(h + 1) * IDX_DIM, :])
            acc = acc + w[h:h + 1, :] * jnp.maximum(d, 0.0)
        causal = rel <= i * tq - c * sc_len
        s = jnp.where(causal, acc, neg_inf)
        sc_ref[c] = s
        mx8 = jnp.maximum(mx8, _fold(s).max(axis=0))
        mn8 = jnp.minimum(mn8, _fold(jnp.where(causal, acc, jnp.inf)).min(axis=0))
        return mx8, mn8

    mx8, mn8 = lax.fori_loop(
        0, n_vis, score_body,
        (jnp.full((FOLD_ROWS, tq), -jnp.inf, F32), jnp.full((FOLD_ROWS, tq), jnp.inf, F32)))
    row_max = mx8.max(axis=0, keepdims=True)
    row_min = mn8.min(axis=0, keepdims=True)

    def count(pred):
        def body(c, cnt8):
            return cnt8 + _fold(jnp.where(pred(c, sc_ref[c]), 1.0, 0.0)).sum(axis=0)
        cnt8 = lax.fori_loop(0, n_vis, body, jnp.zeros((FOLD_ROWS, tq), F32))
        return cnt8.sum(axis=0, keepdims=True)

    def count_ge(t):
        return count(lambda c, s: s >= t)

    n_valid = (i * tq + lane[0:1, :] + 1).astype(F32)
    kk = jnp.minimum(n_valid, float(topk))
    cnt_top = count_ge(row_max)
    top_is_enough = cnt_top >= kk
    lo = jnp.where(top_is_enough, row_max, row_min)
    cnt_lo = jnp.where(top_is_enough, cnt_top, n_valid)
    hi = row_max
    cnt_hi = jnp.where(top_is_enough, 0.0, cnt_top)

    def active(lo, hi, cnt_lo):
        mid = 0.5 * lo + 0.5 * hi
        return (cnt_lo != kk) & (mid > lo) & (mid < hi)

    def any_lane(mask):
        return jnp.max(jnp.where(mask, 1, 0))

    def w_body(carry):
        lo, hi, cnt_lo, cnt_hi, _ = carry
        for _ in range(BISECT_STEPS):
            mid = 0.5 * lo + 0.5 * hi
            act = active(lo, hi, cnt_lo)
            cnt = count_ge(mid)
            up = act & (cnt >= kk)
            down = act & (cnt < kk)
            lo = jnp.where(up, mid, lo)
            cnt_lo = jnp.where(up, cnt, cnt_lo)
            hi = jnp.where(down, mid, hi)
            cnt_hi = jnp.where(down, cnt, cnt_hi)
        return lo, hi, cnt_lo, cnt_hi, any_lane(active(lo, hi, cnt_lo))

    lo, hi, cnt_lo, cnt_hi, _ = lax.while_loop(
        lambda carry: carry[4] > 0, w_body, (lo, hi, cnt_lo, cnt_hi, any_lane(active(lo, hi, cnt_lo))))
    thr = lo

    tie = cnt_lo > kk

    @pl.when(any_lane(tie) > 0)
    def _():
        need = kk - cnt_hi

        def key_index(c):
            return (c * sc_len + row).astype(F32)

        def j_body(_, carry):
            j_lo, j_hi = carry
            mid = jnp.floor(0.5 * (j_lo + j_hi))
            enough = count(lambda c, s: (s == thr) & (key_index(c) <= mid)) >= need
            return jnp.where(enough, j_lo, mid), jnp.where(enough, mid, j_hi)

        _, j_hi = lax.fori_loop(0, idx_steps, j_body, (jnp.full((1, tq), -1.0, F32), n_valid - 1.0))
        j_keep = jnp.where(tie, j_hi, jnp.inf)

        def drop_body(c, carry):
            s = sc_ref[c]
            sc_ref[c] = jnp.where((s == thr) & (key_index(c) > j_keep), neg_inf, s)
            return carry

        lax.fori_loop(0, n_vis, drop_body, 0)

    c_exp = HEAD_DIM ** -0.5 * math.log2(math.e)
    for first in range(0, N_HEADS, ATT_HEADS):
        g = first // rep
        heads = range(first, first + ATT_HEADS)

        def logits_body(c, m8s):
            kch = kc_ref[c][:, g * HEAD_DIM:(g + 1) * HEAD_DIM]
            sel = sc_ref[c] >= thr
            out = []
            for r, hd in enumerate(heads):
                l = _dot(kch, qT_ref[hd * HEAD_DIM:(hd + 1) * HEAD_DIM, :])
                l = jnp.where(sel, l, neg_inf)
                lg_ref[r, c] = l
                out.append(jnp.maximum(m8s[r], _fold(l).max(axis=0)))
            return tuple(out)

        m8s = lax.fori_loop(0, n_vis, logits_body,
                            tuple(jnp.full((FOLD_ROWS, tq), -jnp.inf, F32) for _ in heads))
        ms = [m8.max(axis=0, keepdims=True) for m8 in m8s]
        oacc_ref[...] = jnp.zeros(oacc_ref.shape, F32)

        def pv_body(c, s8s):
            vch = vT_ref[c][g * HEAD_DIM:(g + 1) * HEAD_DIM, :]
            out = []
            for r in range(ATT_HEADS):
                p = jnp.exp2((lg_ref[r, c] - ms[r]) * c_exp)
                oacc_ref[r] += _dot(vch, p.astype(BF16))
                out.append(s8s[r] + _fold(p).sum(axis=0))
            return tuple(out)

        s8s = lax.fori_loop(0, n_vis, pv_body, tuple(jnp.zeros((FOLD_ROWS, tq), F32) for _ in heads))
        for r, hd in enumerate(heads):
            den = s8s[r].sum(axis=0, keepdims=True)
            o_ref[:, hd * HEAD_DIM:(hd + 1) * HEAD_DIM] = (oacc_ref[r] / den).T.astype(BF16)


def _sparse_attention(qT, qiT, wiT, kc, vT, kic, batch, seq_len):
    n = qT.shape[1]
    tq = Q_BLOCK
    assert seq_len % SEQ_CHUNK == 0 and SEQ_CHUNK % tq == 0
    n_q = seq_len // tq
    n_c = seq_len // SEQ_CHUNK
    topk = min(TOPK_MAX, seq_len // 4)
    idx_steps = math.ceil(math.log2(seq_len)) + 1
    qcol = lambda rows: pl.BlockSpec((rows, tq), lambda b, i: (0, b * n_q + i))
    seq = lambda r, c: pl.BlockSpec((n_c, r, c), lambda b, i: (b, 0, 0))
    return pl.pallas_call(
        functools.partial(_dsa_kernel, topk=topk, idx_steps=idx_steps),
        grid=(batch, n_q),
        in_specs=[qcol(ATTN_WIDTH), qcol(IDX_HEADS * IDX_DIM), qcol(IDX_HEADS),
                  seq(SEQ_CHUNK, KV_WIDTH), seq(KV_WIDTH, SEQ_CHUNK), seq(SEQ_CHUNK, LANES)],
        out_specs=pl.BlockSpec((tq, ATTN_WIDTH), lambda b, i: (b * n_q + i, 0)),
        out_shape=jax.ShapeDtypeStruct((n, ATTN_WIDTH), BF16),
        scratch_shapes=[pltpu.VMEM((n_c, SEQ_CHUNK, tq), F32),
                        pltpu.VMEM((ATT_HEADS, n_c, SEQ_CHUNK, tq), F32),
                        pltpu.VMEM((ATT_HEADS, HEAD_DIM, tq), F32)],
        compiler_params=_cparams(("parallel", "arbitrary")),
        name="dsa_attention",
    )(qT, qiT, wiT, kc, vT, kic)


def _gate_merge_kernel(h_ref, c_ref, o_ref, wg0_ref, wg1_ref, wpw_ref, wo_ref, m_ref):
    h = h_ref[...]
    g0 = jax.nn.sigmoid(_dot(h, wg0_ref[...]))
    g1 = jax.nn.sigmoid(_dot(h, wg1_ref[...]))
    y_conv = _dot(c_ref[...], wpw_ref[...].astype(BF16))
    y_attn = _dot(o_ref[...], wo_ref[...].astype(BF16))
    m_ref[...] = (g0 * y_conv + g1 * y_attn).astype(BF16)


def _gate_merge(h, c, o, wg0, wg1, wpw, wo):
    n, d = h.shape
    tm, tn = MERGE_TM, MERGE_TN
    row = lambda width: pl.BlockSpec((tm, width), lambda j, i: (i, 0))
    colw = lambda rows: pl.BlockSpec((rows, tn), lambda j, i: (0, j))
    return pl.pallas_call(
        _gate_merge_kernel,
        grid=(d // tn, n // tm),
        in_specs=[row(d), row(c.shape[1]), row(o.shape[1]),
                  colw(d), colw(d), colw(wpw.shape[0]), colw(wo.shape[0])],
        out_specs=pl.BlockSpec((tm, tn), lambda j, i: (i, j)),
        out_shape=jax.ShapeDtypeStruct((n, d), BF16),
        compiler_params=_cparams(("parallel", "parallel")),
        name="gate_merge",
    )(h, c, o, wg0, wg1, wpw, wo)


def _out_proj_kernel(x_ref, m_ref, wout_ref, gpost_ref, out_ref):
    y = _dot(m_ref[...], wout_ref[...])
    out_ref[...] = x_ref[...] + _rms(y, gpost_ref[...])


def _out_proj(x, merged, wout, gpost):
    n, d = x.shape
    tm = OUT_TM
    row = pl.BlockSpec((tm, d), lambda i: (i, 0))
    return pl.pallas_call(
        _out_proj_kernel,
        grid=(n // tm,),
        in_specs=[row, row, pl.BlockSpec((d, d), lambda i: (0, 0)), pl.BlockSpec((1, d), lambda i: (0, 0))],
        out_specs=row,
        out_shape=jax.ShapeDtypeStruct((n, d), F32),
        compiler_params=_cparams(("parallel",)),
        name="out_proj",
    )(x, merged, wout, gpost)


def _transpose_cast_kernel(w_ref, o_ref):
    o_ref[...] = w_ref[...].T.astype(BF16)


def _transposed_columns(w, layer, col_start, n_cols):
    _, d, width = w.shape
    assert col_start % LANES == 0
    n_blocks = pl.cdiv(n_cols, LANES)
    assert col_start + n_blocks * LANES <= width
    first = col_start // LANES
    return pl.pallas_call(
        _transpose_cast_kernel,
        grid=(n_blocks,),
        in_specs=[pl.BlockSpec((None, d, LANES), lambda j: (layer, 0, first + j))],
        out_specs=pl.BlockSpec((LANES, d), lambda j: (j, 0)),
        out_shape=jax.ShapeDtypeStruct((n_blocks * LANES, d), BF16),
        compiler_params=_cparams(("parallel",)),
        name="transpose_cast",
    )(w)


def _inv_freq(rot_dims):
    half = rot_dims // 2
    return jnp.power(jnp.float32(ROPE_THETA), -jnp.arange(half, dtype=jnp.float32) * (2.0 / rot_dims))


def kernel(x, positions, ffn1_norm_pre, ffn1_w1, ffn1_w2, ffn1_norm_post, mix_norm_pre, w_in, conv_dw, conv_dw_b, conv_ln_g, conv_ln_b, conv_w_pw, attn_w_o, w_out, mix_norm_post, ffn2_norm_pre, ffn2_w1, ffn2_w2, ffn2_norm_post):
    batch, seq_len, d = x.shape
    depth = ffn1_w1.shape[0]
    n = batch * seq_len
    xf = x.reshape(n, d)
    pos_row = positions.reshape(1, n)
    fq = _inv_freq(HEAD_DIM // ROPE_FRACTION_DIV).reshape(-1, 1)
    fi = _inv_freq(IDX_DIM // ROPE_FRACTION_DIV).reshape(-1, 1)

    o_q = 2 * CONV_WIDTH
    o_k = o_q + ATTN_WIDTH
    o_v = o_k + KV_WIDTH
    o_qi = o_v + KV_WIDTH
    o_ki = o_qi + IDX_HEADS * IDX_DIM
    o_wi = o_ki + IDX_DIM
    o_g = o_wi + IDX_HEADS

    for l in range(depth):
        win = w_in[l]
        wab = win[:, :o_q].astype(BF16)
        wt = _transposed_columns(w_in, l, o_q, o_g - o_q)
        wg0 = win[:, o_g:o_g + d].astype(BF16)
        wg1 = win[:, o_g + d:o_g + 2 * d].astype(BF16)

        xf, h = _ffn(xf, ffn1_norm_pre[l][None], ffn1_w1[l], ffn1_w2[l],
                     ffn1_norm_post[l][None], mix_norm_pre[l][None])
        c, qT, qiT, wiT, kc, vT, kic = _mixer_inputs(
            h, pos_row, wab, wt, fq, fi, conv_dw[l], conv_dw_b[l][None], conv_ln_g[l][None], conv_ln_b[l][None],
            seq_len)
        o = _sparse_attention(qT, qiT, wiT, kc, vT, kic, batch, seq_len)
        merged = _gate_merge(h, c, o, wg0, wg1, conv_w_pw[l], attn_w_o[l])
        xf = _out_proj(xf, merged, w_out[l].astype(BF16), mix_norm_post[l][None])
        xf = _ffn(xf, ffn2_norm_pre[l][None], ffn2_w1[l], ffn2_w2[l], ffn2_norm_post[l][None])
    return xf.reshape(batch, seq_len, d)
```

```python
import functools
import math

import jax
import jax.numpy as jnp
from jax import lax
from jax.experimental import pallas as pl
from jax.experimental.pallas import tpu as pltpu

F32 = jnp.float32
BF16 = jnp.bfloat16

CONV_WIDTH = 1024
CONV_KERNEL = 31
N_HEADS = 8
N_KV_HEADS = 2
N_BRANCHES = 2
HEAD_DIM = 128
ATTN_WIDTH = N_HEADS * HEAD_DIM
KV_WIDTH = N_KV_HEADS * HEAD_DIM
IDX_HEADS = 16
IDX_DIM = 64
TOPK_MAX = 256
ROPE_THETA = 500000.0
ROPE_FRACTION_DIV = 4
NORM_EPS = 1e-6

V7X_VMEM_BYTES = 64 * 1024 * 1024
VMEM_LIMIT_BYTES = V7X_VMEM_BYTES - 3 * 1024 * 1024

SUBLANES = 8
LANES = 128

FFN_TM = 1024
FFN_TF = 256
MIX_TM = 512
CONV_RC = 32
CONV_LW = 512
CONV_HALO = 32
CONV_LN_ROWS = 128
SEQ_CHUNK = 512
Q_BLOCK = 512
ATT_HEADS = 2
BISECT_STEPS = 4
MERGE_TM = 1024
MERGE_TN = 512
OUT_TM = 512
CAST_ROWS = 256

_NT = (((1,), (1,)), ((), ()))


def _cparams(sem):
    return pltpu.CompilerParams(dimension_semantics=sem, vmem_limit_bytes=VMEM_LIMIT_BYTES)


def _rms(x, gain):
    ms = jnp.mean(x * x, axis=-1, keepdims=True)
    return x * lax.rsqrt(ms + NORM_EPS) * gain


def _dot(a, b):
    return jnp.dot(a, b, preferred_element_type=F32)


def _ffn_kernel(*refs, nj, emit_h):
    if emit_h:
        x_ref, gpre_ref, w1g_ref, w1u_ref, w2_ref, gpost_ref, gnext_ref, o_ref, h_ref, hn_ref = refs
    else:
        x_ref, gpre_ref, w1g_ref, w1u_ref, w2_ref, gpost_ref, o_ref, hn_ref = refs
    j = pl.program_id(1)

    @pl.when(j == 0)
    def _():
        hn_ref[...] = _rms(x_ref[...], gpre_ref[...]).astype(BF16)
        o_ref[...] = jnp.zeros(o_ref.shape, F32)

    hn = hn_ref[...]
    g = _dot(hn, w1g_ref[...].astype(BF16))
    u = _dot(hn, w1u_ref[...].astype(BF16))
    a = (g * jax.nn.sigmoid(g) * u).astype(BF16)
    o_ref[...] += _dot(a, w2_ref[...].astype(BF16))

    @pl.when(j == nj - 1)
    def _():
        out = x_ref[...] + 0.5 * _rms(o_ref[...], gpost_ref[...])
        o_ref[...] = out
        if emit_h:
            h_ref[...] = _rms(out, gnext_ref[...]).astype(BF16)


def _ffn(x, gpre, w1, w2, gpost, gnext=None):
    n, d = x.shape
    f = w2.shape[0]
    tm, tf = FFN_TM, FFN_TF
    nj = f // tf
    assert n % tm == 0 and f % tf == 0
    emit_h = gnext is not None
    row = pl.BlockSpec((tm, d), lambda i, j: (i, 0))
    vec = pl.BlockSpec((1, d), lambda i, j: (0, 0))
    in_specs = [row, vec,
                pl.BlockSpec((d, tf), lambda i, j: (0, j)),
                pl.BlockSpec((d, tf), lambda i, j: (0, j + nj)),
                pl.BlockSpec((tf, d), lambda i, j: (j, 0)),
                vec]
    args = [x, gpre, w1, w1, w2, gpost]
    out_shape = [jax.ShapeDtypeStruct((n, d), F32)]
    out_specs = [row]
    if emit_h:
        in_specs.append(vec)
        args.append(gnext)
        out_shape.append(jax.ShapeDtypeStruct((n, d), BF16))
        out_specs.append(row)
    res = pl.pallas_call(
        functools.partial(_ffn_kernel, nj=nj, emit_h=emit_h),
        grid=(n // tm, nj),
        in_specs=in_specs,
        out_specs=out_specs,
        out_shape=out_shape,
        scratch_shapes=[pltpu.VMEM((tm, d), BF16)],
        compiler_params=_cparams(("parallel", "arbitrary")),
        name="ffn_emit_h" if emit_h else "ffn",
    )(*args)
    return res if emit_h else res[0]


def _mixer_in_kernel(h_ref, pos_ref, wab_ref, wt_ref, fq_ref, fi_ref, dw_ref, dwb_ref, lng_ref, lnb_ref,
                     c_ref, qT_ref, qiT_ref, wiT_ref, kc_ref, vT_ref, kic_ref, sh_ref, pre_ref, *, blocks_per_seq):
    tm = h_ref.shape[0]
    cw = c_ref.shape[1]
    halo = CONV_HALO
    i = pl.program_id(0)
    first = (i % blocks_per_seq) == 0

    @pl.when(first)
    def _():
        sh_ref[0, 0:halo, :] = jnp.zeros((halo, cw), F32)

    @pl.when(jnp.logical_not(first))
    def _():
        sh_ref[0, 0:halo, :] = sh_ref[0, tm:tm + halo, :]

    z = _dot(h_ref[...], wab_ref[...])
    sh_ref[0, halo:halo + tm, :] = z[:, :cw] * jax.nn.sigmoid(z[:, cw:])

    span = tm + halo - SUBLANES
    for s in range(1, SUBLANES):
        sh_ref[s, 0:span, :] = sh_ref[0, s:s + span, :]

    first_tap = halo - (CONV_KERNEL - 1)

    def taps(r0):
        for lt in range(cw // CONV_LW):
            lanes = slice(lt * CONV_LW, (lt + 1) * CONV_LW)
            acc = jnp.zeros((CONV_RC // SUBLANES, SUBLANES, CONV_LW), F32)
            for k in range(CONV_KERNEL):
                off = first_tap + k
                s, base = off % SUBLANES, off - off % SUBLANES
                x = sh_ref[s, r0 + base:r0 + base + CONV_RC, lanes]
                acc = acc + dw_ref[k, :, lanes][None] * x.reshape(acc.shape)
            pre_ref[r0:r0 + CONV_RC, lanes] = acc.reshape(CONV_RC, CONV_LW)

    stages = _proj_stages(h_ref, pos_ref, wt_ref, fq_ref, fi_ref, qT_ref, qiT_ref, wiT_ref, kc_ref, vT_ref, kic_ref)
    chunks = list(range(0, tm, CONV_RC))
    share = len(chunks) // len(stages)
    for si, stage in enumerate(stages):
        stage()
        last = len(chunks) if si == len(stages) - 1 else (si + 1) * share
        for r0 in chunks[si * share:last]:
            taps(r0)

    def norm(rb, carry):
        r0 = pl.multiple_of(rb * CONV_LN_ROWS, CONV_LN_ROWS)
        acc = pre_ref[pl.ds(r0, CONV_LN_ROWS), :] + dwb_ref[...]
        mu = jnp.mean(acc, axis=-1, keepdims=True)
        xc = acc - mu
        var = jnp.mean(xc * xc, axis=-1, keepdims=True)
        y = xc * lax.rsqrt(var + NORM_EPS) * lng_ref[...] + lnb_ref[...]
        c_ref[pl.ds(r0, CONV_LN_ROWS), :] = (y * jax.nn.sigmoid(y)).astype(BF16)
        return carry

    lax.fori_loop(0, tm // CONV_LN_ROWS, norm, 0)


def _mixer_inputs(h, pos_row, wab, wt, fq, fi, dw, dwb, lng, lnb, seq_len):
    n, d = h.shape
    cw = dw.shape[1]
    tm = MIX_TM
    assert seq_len % tm == 0 and tm % SEQ_CHUNK == 0 and CONV_HALO >= CONV_KERNEL - 1
    cpb = tm // SEQ_CHUNK
    n_chunks = n // SEQ_CHUNK
    vec = pl.BlockSpec((1, cw), lambda i: (0, 0))
    col = lambda rows: pl.BlockSpec((rows, tm), lambda i: (0, i))
    const = lambda shape: pl.BlockSpec(shape, lambda i: (0, 0), pipeline_mode=pl.Buffered(1))
    dw_rows = jnp.broadcast_to(dw[:, None, :], (CONV_KERNEL, SUBLANES, cw))
    return pl.pallas_call(
        functools.partial(_mixer_in_kernel, blocks_per_seq=seq_len // tm),
        grid=(n // tm,),
        in_specs=[pl.BlockSpec((tm, d), lambda i: (i, 0)),
                  pl.BlockSpec((1, tm), lambda i: (0, i)),
                  const((d, 2 * cw)), const((_ROWS, d)),
                  pl.BlockSpec(fq.shape, lambda i: (0, 0)),
                  pl.BlockSpec(fi.shape, lambda i: (0, 0)),
                  pl.BlockSpec((CONV_KERNEL, SUBLANES, cw), lambda i: (0, 0, 0)),
                  vec, vec, vec],
        out_specs=[pl.BlockSpec((tm, cw), lambda i: (i, 0)),
                   col(ATTN_WIDTH), col(IDX_HEADS * IDX_DIM), col(IDX_HEADS),
                   pl.BlockSpec((cpb, SEQ_CHUNK, KV_WIDTH), lambda i: (i, 0, 0)),
                   pl.BlockSpec((cpb, KV_WIDTH, SEQ_CHUNK), lambda i: (i, 0, 0)),
                   pl.BlockSpec((cpb, SEQ_CHUNK, LANES), lambda i: (i, 0, 0))],
        out_shape=[jax.ShapeDtypeStruct((n, cw), BF16),
                   jax.ShapeDtypeStruct((ATTN_WIDTH, n), BF16),
                   jax.ShapeDtypeStruct((IDX_HEADS * IDX_DIM, n), BF16),
                   jax.ShapeDtypeStruct((IDX_HEADS, n), F32),
                   jax.ShapeDtypeStruct((n_chunks, SEQ_CHUNK, KV_WIDTH), BF16),
                   jax.ShapeDtypeStruct((n_chunks, KV_WIDTH, SEQ_CHUNK), BF16),
                   jax.ShapeDtypeStruct((n_chunks, SEQ_CHUNK, LANES), BF16)],
        scratch_shapes=[pltpu.VMEM((SUBLANES, tm + CONV_HALO, cw), F32),
                        pltpu.VMEM((tm, cw), F32)],
        compiler_params=_cparams(("arbitrary",)),
        name="mixer_inputs",
    )(h, pos_row, wab, wt, fq, fi, dw_rows, dwb, lng, lnb)


_ROW_Q = 0
_ROW_K = _ROW_Q + ATTN_WIDTH
_ROW_V = _ROW_K + KV_WIDTH
_ROW_QI = _ROW_V + KV_WIDTH
_ROW_KI = _ROW_QI + IDX_HEADS * IDX_DIM
_ROW_WI = _ROW_KI + IDX_DIM
_ROWS = _ROW_WI + IDX_HEADS


def _rope_rows(x, cos, sin, half):
    x1, x2 = x[0:half], x[half:2 * half]
    return jnp.concatenate([x1 * cos - x2 * sin, x2 * cos + x1 * sin, x[2 * half:]], axis=0)


def _proj_stages(h_ref, pos_ref, wt_ref, fq_ref, fi_ref, qT_ref, qiT_ref, wiT_ref, kc_ref, vT_ref, kic_ref):
    tm = h_ref.shape[0]
    n_chunk = tm // SEQ_CHUNK
    hq = HEAD_DIM // ROPE_FRACTION_DIV // 2
    hi = IDX_DIM // ROPE_FRACTION_DIV // 2

    def rot(f_ref):
        ang = f_ref[...] * pos_ref[...].astype(F32)
        return jnp.cos(ang), jnp.sin(ang)

    def proj(lo, hi_):
        return lax.dot_general(wt_ref[lo:hi_, :], h_ref[...], _NT, preferred_element_type=F32)

    def queries():
        cq, sq = rot(fq_ref)
        zq = proj(_ROW_Q, _ROW_K)
        for hd in range(N_HEADS):
            b = hd * HEAD_DIM
            qT_ref[b:b + HEAD_DIM, :] = _rope_rows(zq[b:b + HEAD_DIM], cq, sq, hq).astype(BF16)

    def keys_values():
        cq, sq = rot(fq_ref)
        zk = proj(_ROW_K, _ROW_V)
        kT = jnp.concatenate(
            [_rope_rows(zk[g * HEAD_DIM:(g + 1) * HEAD_DIM], cq, sq, hq) for g in range(N_KV_HEADS)], axis=0)
        kn = kT.T
        zv = proj(_ROW_V, _ROW_QI)
        for c in range(n_chunk):
            kc_ref[c] = kn[c * SEQ_CHUNK:(c + 1) * SEQ_CHUNK, :].astype(BF16)
            vT_ref[c] = zv[:, c * SEQ_CHUNK:(c + 1) * SEQ_CHUNK].astype(BF16)

    def indexer_queries():
        ci, si = rot(fi_ref)
        zqi = proj(_ROW_QI, _ROW_KI)
        for hd in range(IDX_HEADS):
            b = hd * IDX_DIM
            qiT_ref[b:b + IDX_DIM, :] = _rope_rows(zqi[b:b + IDX_DIM], ci, si, hi).astype(BF16)

    def indexer_keys():
        ci, si = rot(fi_ref)
        zr = proj(_ROW_KI, _ROWS)
        kiT = jnp.concatenate(
            [_rope_rows(zr[0:IDX_DIM], ci, si, hi), jnp.zeros((LANES - IDX_DIM, tm), F32)], axis=0)
        kin = kiT.T
        for c in range(n_chunk):
            kic_ref[c] = kin[c * SEQ_CHUNK:(c + 1) * SEQ_CHUNK, :].astype(BF16)
        wiT_ref[...] = zr[IDX_DIM:IDX_DIM + IDX_HEADS] * (IDX_HEADS ** -0.5 * IDX_DIM ** -0.5)

    return [queries, keys_values, indexer_queries, indexer_keys]


FOLD_ROWS = 4 * SUBLANES


def _fold(x):
    return x.reshape(x.shape[0] // FOLD_ROWS, FOLD_ROWS, x.shape[1])


def _dsa_kernel(qT_ref, qiT_ref, wiT_ref, kc_ref, vT_ref, kic_ref, o_ref, sc_ref, lg_ref, oacc_ref, *,
                topk, idx_steps):
    sc_len, tq = sc_ref.shape[1], sc_ref.shape[2]
    i = pl.program_id(1)
    n_vis = (i * tq) // sc_len + 1
    rep = N_HEADS // N_KV_HEADS
    neg_inf = jnp.float32(-jnp.inf)
    row = lax.broadcasted_iota(jnp.int32, (sc_len, tq), 0)
    lane = lax.broadcasted_iota(jnp.int32, (sc_len, tq), 1)
    rel = row - lane
    w = wiT_ref[...]

    def score_body(c, carry):
        mx8, mn8 = carry
        kch = kic_ref[c][:, :IDX_DIM]
        acc = jnp.zeros((sc_len, tq), F32)
        for h in range(IDX_HEADS):
            d = _dot(kch, qiT_ref[h * IDX_DIM:(h + 1) * IDX_DIM, :])
            acc = acc + w[h:h + 1, :] * jnp.maximum(d, 0.0)
        causal = rel <= i * tq - c * sc_len
        s = jnp.where(causal, acc, neg_inf)
        sc_ref[c] = s
        mx8 = jnp.maximum(mx8, _fold(s).max(axis=0))
        mn8 = jnp.minimum(mn8, _fold(jnp.where(causal, acc, jnp.inf)).min(axis=0))
        return mx8, mn8

    mx8, mn8 = lax.fori_loop(
        0, n_vis, score_body,
        (jnp.full((FOLD_ROWS, tq), -jnp.inf, F32), jnp.full((FOLD_ROWS, tq), jnp.inf, F32)))
    row_max = mx8.max(axis=0, keepdims=True)
    row_min = mn8.min(axis=0, keepdims=True)

    def count(pred):
        def body(c, cnt8):
            return cnt8 + _fold(jnp.where(pred(c, sc_ref[c]), 1.0, 0.0)).sum(axis=0)
        cnt8 = lax.fori_loop(0, n_vis, body, jnp.zeros((FOLD_ROWS, tq), F32))
        return cnt8.sum(axis=0, keepdims=True)

    def count_ge(t):
        return count(lambda c, s: s >= t)

    n_valid = (i * tq + lane[0:1, :] + 1).astype(F32)
    kk = jnp.minimum(n_valid, float(topk))
    cnt_top = count_ge(row_max)
    top_is_enough = cnt_top >= kk
    lo = jnp.where(top_is_enough, row_max, row_min)
    cnt_lo = jnp.where(top_is_enough, cnt_top, n_valid)
    hi = row_max
    cnt_hi = jnp.where(top_is_enough, 0.0, cnt_top)

    def active(lo, hi, cnt_lo):
        mid = 0.5 * lo + 0.5 * hi
        return (cnt_lo != kk) & (mid > lo) & (mid < hi)

    def any_lane(mask):
        return jnp.max(jnp.where(mask, 1, 0))

    def w_body(carry):
        lo, hi, cnt_lo, cnt_hi, _ = carry
        for _ in range(BISECT_STEPS):
            mid = 0.5 * lo + 0.5 * hi
            act = active(lo, hi, cnt_lo)
            cnt = count_ge(mid)
            up = act & (cnt >= kk)
            down = act & (cnt < kk)
            lo = jnp.where(up, mid, lo)
            cnt_lo = jnp.where(up, cnt, cnt_lo)
            hi = jnp.where(down, mid, hi)
            cnt_hi = jnp.where(down, cnt, cnt_hi)
        return lo, hi, cnt_lo, cnt_hi, any_lane(active(lo, hi, cnt_lo))

    lo, hi, cnt_lo, cnt_hi, _ = lax.while_loop(
        lambda carry: carry[4] > 0, w_body, (lo, hi, cnt_lo, cnt_hi, any_lane(active(lo, hi, cnt_lo))))
    thr = lo

    tie = cnt_lo > kk

    @pl.when(any_lane(tie) > 0)
    def _():
        need = kk - cnt_hi

        def key_index(c):
            return (c * sc_len + row).astype(F32)

        def j_body(_, carry):
            j_lo, j_hi = carry
            mid = jnp.floor(0.5 * (j_lo + j_hi))
            enough = count(lambda c, s: (s == thr) & (key_index(c) <= mid)) >= need
            return jnp.where(enough, j_lo, mid), jnp.where(enough, mid, j_hi)

        _, j_hi = lax.fori_loop(0, idx_steps, j_body, (jnp.full((1, tq), -1.0, F32), n_valid - 1.0))
        j_keep = jnp.where(tie, j_hi, jnp.inf)

        def drop_body(c, carry):
            s = sc_ref[c]
            sc_ref[c] = jnp.where((s == thr) & (key_index(c) > j_keep), neg_inf, s)
            return carry

        lax.fori_loop(0, n_vis, drop_body, 0)

    c_exp = HEAD_DIM ** -0.5 * math.log2(math.e)
    for first in range(0, N_HEADS, ATT_HEADS):
        g = first // rep
        heads = range(first, first + ATT_HEADS)

        def logits_body(c, m8s):
            kch = kc_ref[c][:, g * HEAD_DIM:(g + 1) * HEAD_DIM]
            sel = sc_ref[c] >= thr
            out = []
            for r, hd in enumerate(heads):
                l = _dot(kch, qT_ref[hd * HEAD_DIM:(hd + 1) * HEAD_DIM, :])
                l = jnp.where(sel, l, neg_inf)
                lg_ref[r, c] = l
                out.append(jnp.maximum(m8s[r], _fold(l).max(axis=0)))
            return tuple(out)

        m8s = lax.fori_loop(0, n_vis, logits_body,
                            tuple(jnp.full((FOLD_ROWS, tq), -jnp.inf, F32) for _ in heads))
        ms = [m8.max(axis=0, keepdims=True) for m8 in m8s]
        oacc_ref[...] = jnp.zeros(oacc_ref.shape, F32)

        def pv_body(c, s8s):
            vch = vT_ref[c][g * HEAD_DIM:(g + 1) * HEAD_DIM, :]
            out = []
            for r in range(ATT_HEADS):
                p = jnp.exp2((lg_ref[r, c] - ms[r]) * c_exp)
                oacc_ref[r] += _dot(vch, p.astype(BF16))
                out.append(s8s[r] + _fold(p).sum(axis=0))
            return tuple(out)

        s8s = lax.fori_loop(0, n_vis, pv_body, tuple(jnp.zeros((FOLD_ROWS, tq), F32) for _ in heads))
        for r, hd in enumerate(heads):
            den = s8s[r].sum(axis=0, keepdims=True)
            o_ref[:, hd * HEAD_DIM:(hd + 1) * HEAD_DIM] = (oacc_ref[r] / den).T.astype(BF16)


def _sparse_attention(qT, qiT, wiT, kc, vT, kic, batch, seq_len):
    n = qT.shape[1]
    tq = Q_BLOCK
    assert seq_len % SEQ_CHUNK == 0 and SEQ_CHUNK % tq == 0
    n_q = seq_len // tq
    n_c = seq_len // SEQ_CHUNK
    topk = min(TOPK_MAX, seq_len // 4)
    idx_steps = math.ceil(math.log2(seq_len)) + 1
    qcol = lambda rows: pl.BlockSpec((rows, tq), lambda b, i: (0, b * n_q + i))
    seq = lambda r, c: pl.BlockSpec((n_c, r, c), lambda b, i: (b, 0, 0))
    return pl.pallas_call(
        functools.partial(_dsa_kernel, topk=topk, idx_steps=idx_steps),
        grid=(batch, n_q),
        in_specs=[qcol(ATTN_WIDTH), qcol(IDX_HEADS * IDX_DIM), qcol(IDX_HEADS),
                  seq(SEQ_CHUNK, KV_WIDTH), seq(KV_WIDTH, SEQ_CHUNK), seq(SEQ_CHUNK, LANES)],
        out_specs=pl.BlockSpec((tq, ATTN_WIDTH), lambda b, i: (b * n_q + i, 0)),
        out_shape=jax.ShapeDtypeStruct((n, ATTN_WIDTH), BF16),
        scratch_shapes=[pltpu.VMEM((n_c, SEQ_CHUNK, tq), F32),
                        pltpu.VMEM((ATT_HEADS, n_c, SEQ_CHUNK, tq), F32),
                        pltpu.VMEM((ATT_HEADS, HEAD_DIM, tq), F32)],
        compiler_params=_cparams(("parallel", "arbitrary")),
        name="dsa_attention",
    )(qT, qiT, wiT, kc, vT, kic)


def _gate_merge_kernel(h_ref, c_ref, o_ref, wg0_ref, wg1_ref, wpw_ref, wo_ref, m_ref):
    h = h_ref[...]
    g0 = jax.nn.sigmoid(_dot(h, wg0_ref[...]))
    g1 = jax.nn.sigmoid(_dot(h, wg1_ref[...]))
    y_conv = _dot(c_ref[...], wpw_ref[...].astype(BF16))
    y_attn = _dot(o_ref[...], wo_ref[...].astype(BF16))
    m_ref[...] = (g0 * y_conv + g1 * y_attn).astype(BF16)


def _gate_merge(h, c, o, wg, wpw, wo):
    n, d = h.shape
    tm, tn = MERGE_TM, MERGE_TN
    nj = d // tn
    row = lambda width: pl.BlockSpec((tm, width), lambda j, i: (i, 0))
    colw = lambda rows: pl.BlockSpec((rows, tn), lambda j, i: (0, j))
    return pl.pallas_call(
        _gate_merge_kernel,
        grid=(nj, n // tm),
        in_specs=[row(d), row(c.shape[1]), row(o.shape[1]),
                  colw(d), pl.BlockSpec((d, tn), lambda j, i: (0, j + nj)),
                  colw(wpw.shape[0]), colw(wo.shape[0])],
        out_specs=pl.BlockSpec((tm, tn), lambda j, i: (i, j)),
        out_shape=jax.ShapeDtypeStruct((n, d), BF16),
        compiler_params=_cparams(("parallel", "parallel")),
        name="gate_merge",
    )(h, c, o, wg, wg, wpw, wo)


def _out_proj_kernel(x_ref, m_ref, wout_ref, gpost_ref, out_ref):
    y = _dot(m_ref[...], wout_ref[...])
    out_ref[...] = x_ref[...] + _rms(y, gpost_ref[...])


def _out_proj(x, merged, wout, gpost):
    n, d = x.shape
    tm = OUT_TM
    row = pl.BlockSpec((tm, d), lambda i: (i, 0))
    return pl.pallas_call(
        _out_proj_kernel,
        grid=(n // tm,),
        in_specs=[row, row, pl.BlockSpec((d, d), lambda i: (0, 0)), pl.BlockSpec((1, d), lambda i: (0, 0))],
        out_specs=row,
        out_shape=jax.ShapeDtypeStruct((n, d), F32),
        compiler_params=_cparams(("parallel",)),
        name="out_proj",
    )(x, merged, wout, gpost)


def _transpose_cast_kernel(w_ref, o_ref):
    o_ref[...] = w_ref[...].T.astype(BF16)


def _transposed_columns(w, layer, col_start, n_cols):
    _, d, width = w.shape
    assert col_start % LANES == 0
    n_blocks = pl.cdiv(n_cols, LANES)
    assert col_start + n_blocks * LANES <= width
    first = col_start // LANES
    return pl.pallas_call(
        _transpose_cast_kernel,
        grid=(n_blocks,),
        in_specs=[pl.BlockSpec((None, d, LANES), lambda j: (layer, 0, first + j))],
        out_specs=pl.BlockSpec((LANES, d), lambda j: (j, 0)),
        out_shape=jax.ShapeDtypeStruct((n_blocks * LANES, d), BF16),
        compiler_params=_cparams(("parallel",)),
        name="transpose_cast",
    )(w)


def _cast_columns_kernel(w_ref, o_ref, *, skip):
    o_ref[...] = w_ref[:, skip:skip + o_ref.shape[1]].astype(BF16)


def _cast_columns(w, layer, col_start, n_cols):
    _, d, width = w.shape
    skip = col_start % LANES
    base = col_start - skip
    block_w = base if base else pl.cdiv(n_cols, LANES) * LANES
    assert skip + n_cols <= block_w, "the column range must fit one input block"
    tr = CAST_ROWS
    return pl.pallas_call(
        functools.partial(_cast_columns_kernel, skip=skip),
        grid=(d // tr,),
        in_specs=[pl.BlockSpec((None, tr, block_w), lambda i: (layer, i, 1 if base else 0))],
        out_specs=pl.BlockSpec((tr, n_cols), lambda i: (i, 0)),
        out_shape=jax.ShapeDtypeStruct((d, n_cols), BF16),
        compiler_params=_cparams(("parallel",)),
        name="cast_columns",
    )(w)


def _inv_freq(rot_dims):
    half = rot_dims // 2
    return jnp.power(jnp.float32(ROPE_THETA), -jnp.arange(half, dtype=jnp.float32) * (2.0 / rot_dims))


def kernel(x, positions, ffn1_norm_pre, ffn1_w1, ffn1_w2, ffn1_norm_post, mix_norm_pre, w_in, conv_dw, conv_dw_b, conv_ln_g, conv_ln_b, conv_w_pw, attn_w_o, w_out, mix_norm_post, ffn2_norm_pre, ffn2_w1, ffn2_w2, ffn2_norm_post):
    batch, seq_len, d = x.shape
    depth = ffn1_w1.shape[0]
    n = batch * seq_len
    xf = x.reshape(n, d)
    pos_row = positions.reshape(1, n)
    fq = _inv_freq(HEAD_DIM // ROPE_FRACTION_DIV).reshape(-1, 1)
    fi = _inv_freq(IDX_DIM // ROPE_FRACTION_DIV).reshape(-1, 1)

    o_q = 2 * CONV_WIDTH
    o_k = o_q + ATTN_WIDTH
    o_v = o_k + KV_WIDTH
    o_qi = o_v + KV_WIDTH
    o_ki = o_qi + IDX_HEADS * IDX_DIM
    o_wi = o_ki + IDX_DIM
    o_g = o_wi + IDX_HEADS

    for l in range(depth):
        wab = _cast_columns(w_in, l, 0, o_q)
        wt = _transposed_columns(w_in, l, o_q, o_g - o_q)
        wg = _cast_columns(w_in, l, o_g, N_BRANCHES * d)

        xf, h = _ffn(xf, ffn1_norm_pre[l][None], ffn1_w1[l], ffn1_w2[l],
                     ffn1_norm_post[l][None], mix_norm_pre[l][None])
        c, qT, qiT, wiT, kc, vT, kic = _mixer_inputs(
            h, pos_row, wab, wt, fq, fi, conv_dw[l], conv_dw_b[l][None], conv_ln_g[l][None], conv_ln_b[l][None],
            seq_len)
        o = _sparse_attention(qT, qiT, wiT, kc, vT, kic, batch, seq_len)
        merged = _gate_merge(h, c, o, wg, conv_w_pw[l], attn_w_o[l])
        xf = _out_proj(xf, merged, w_out[l].astype(BF16), mix_norm_post[l][None])
        xf = _ffn(xf, ffn2_norm_pre[l][None], ffn2_w1[l], ffn2_w2[l], ffn2_norm_post[l][None])
    return xf.reshape(batch, seq_len, d)
```

```python
import functools
import math

import jax
import jax.numpy as jnp
from jax import lax
from jax.experimental import pallas as pl
from jax.experimental.pallas import tpu as pltpu

F32 = jnp.float32
BF16 = jnp.bfloat16

CONV_WIDTH = 1024
CONV_KERNEL = 31
N_HEADS = 8
N_KV_HEADS = 2
N_BRANCHES = 2
HEAD_DIM = 128
ATTN_WIDTH = N_HEADS * HEAD_DIM
KV_WIDTH = N_KV_HEADS * HEAD_DIM
IDX_HEADS = 16
IDX_DIM = 64
TOPK_MAX = 256
ROPE_THETA = 500000.0
ROPE_FRACTION_DIV = 4
NORM_EPS = 1e-6

V7X_VMEM_BYTES = 64 * 1024 * 1024
VMEM_LIMIT_BYTES = V7X_VMEM_BYTES - 3 * 1024 * 1024

SUBLANES = 8
LANES = 128

FFN_TM = 1024
FFN_TF = 256
MIX_TM = 512
CONV_RC = 32
CONV_LW = 512
CONV_HALO = 32
CONV_LN_ROWS = 128
SEQ_CHUNK = 512
Q_BLOCK = 512
ATT_HEADS = 2
BISECT_STEPS = 4
MERGE_TM = 1024
MERGE_TN = 512
OUT_TM = 512
CAST_ROWS = 976

_NT = (((1,), (1,)), ((), ()))


def _cparams(sem):
    return pltpu.CompilerParams(dimension_semantics=sem, vmem_limit_bytes=VMEM_LIMIT_BYTES)


def _rms(x, gain):
    ms = jnp.mean(x * x, axis=-1, keepdims=True)
    return x * lax.rsqrt(ms + NORM_EPS) * gain


def _dot(a, b):
    return jnp.dot(a, b, preferred_element_type=F32)


def _ffn_kernel(*refs, nj, emit_h):
    if emit_h:
        x_ref, gpre_ref, w1g_ref, w1u_ref, w2_ref, gpost_ref, gnext_ref, o_ref, h_ref, hn_ref = refs
    else:
        x_ref, gpre_ref, w1g_ref, w1u_ref, w2_ref, gpost_ref, o_ref, hn_ref = refs
    j = pl.program_id(1)

    @pl.when(j == 0)
    def _():
        hn_ref[...] = _rms(x_ref[...], gpre_ref[...]).astype(BF16)
        o_ref[...] = jnp.zeros(o_ref.shape, F32)

    hn = hn_ref[...]
    g = _dot(hn, w1g_ref[...].astype(BF16))
    u = _dot(hn, w1u_ref[...].astype(BF16))
    a = (g * jax.nn.sigmoid(g) * u).astype(BF16)
    o_ref[...] += _dot(a, w2_ref[...].astype(BF16))

    @pl.when(j == nj - 1)
    def _():
        out = x_ref[...] + 0.5 * _rms(o_ref[...], gpost_ref[...])
        o_ref[...] = out
        if emit_h:
            h_ref[...] = _rms(out, gnext_ref[...]).astype(BF16)


def _ffn(x, gpre, w1, w2, gpost, gnext=None):
    n, d = x.shape
    f = w2.shape[0]
    tm, tf = FFN_TM, FFN_TF
    nj = f // tf
    assert n % tm == 0 and f % tf == 0
    emit_h = gnext is not None
    row = pl.BlockSpec((tm, d), lambda i, j: (i, 0))
    vec = pl.BlockSpec((1, d), lambda i, j: (0, 0))
    in_specs = [row, vec,
                pl.BlockSpec((d, tf), lambda i, j: (0, j)),
                pl.BlockSpec((d, tf), lambda i, j: (0, j + nj)),
                pl.BlockSpec((tf, d), lambda i, j: (j, 0)),
                vec]
    args = [x, gpre, w1, w1, w2, gpost]
    out_shape = [jax.ShapeDtypeStruct((n, d), F32)]
    out_specs = [row]
    if emit_h:
        in_specs.append(vec)
        args.append(gnext)
        out_shape.append(jax.ShapeDtypeStruct((n, d), BF16))
        out_specs.append(row)
    res = pl.pallas_call(
        functools.partial(_ffn_kernel, nj=nj, emit_h=emit_h),
        grid=(n // tm, nj),
        in_specs=in_specs,
        out_specs=out_specs,
        out_shape=out_shape,
        scratch_shapes=[pltpu.VMEM((tm, d), BF16)],
        compiler_params=_cparams(("parallel", "arbitrary")),
        name="ffn_emit_h" if emit_h else "ffn",
    )(*args)
    return res if emit_h else res[0]


def _mixer_in_kernel(h_ref, pos_ref, wab_ref, wt_ref, fq_ref, fi_ref, dw_ref, dwb_ref, lng_ref, lnb_ref,
                     c_ref, qT_ref, qiT_ref, wiT_ref, kc_ref, vT_ref, kic_ref, sh_ref, pre_ref, *, blocks_per_seq):
    tm = h_ref.shape[0]
    cw = c_ref.shape[1]
    halo = CONV_HALO
    i = pl.program_id(0)
    first = (i % blocks_per_seq) == 0

    @pl.when(first)
    def _():
        sh_ref[0, 0:halo, :] = jnp.zeros((halo, cw), F32)

    @pl.when(jnp.logical_not(first))
    def _():
        sh_ref[0, 0:halo, :] = sh_ref[0, tm:tm + halo, :]

    z = lax.dot_general(h_ref[...], wab_ref[...], _NT, preferred_element_type=F32)
    sh_ref[0, halo:halo + tm, :] = z[:, :cw] * jax.nn.sigmoid(z[:, cw:])

    span = tm + halo - SUBLANES
    for s in range(1, SUBLANES):
        sh_ref[s, 0:span, :] = sh_ref[0, s:s + span, :]

    first_tap = halo - (CONV_KERNEL - 1)

    def taps(r0):
        for lt in range(cw // CONV_LW):
            lanes = slice(lt * CONV_LW, (lt + 1) * CONV_LW)
            acc = jnp.zeros((CONV_RC // SUBLANES, SUBLANES, CONV_LW), F32)
            for k in range(CONV_KERNEL):
                off = first_tap + k
                s, base = off % SUBLANES, off - off % SUBLANES
                x = sh_ref[s, r0 + base:r0 + base + CONV_RC, lanes]
                acc = acc + dw_ref[k, :, lanes][None] * x.reshape(acc.shape)
            pre_ref[r0:r0 + CONV_RC, lanes] = acc.reshape(CONV_RC, CONV_LW)

    stages = _proj_stages(h_ref, pos_ref, wt_ref, fq_ref, fi_ref, qT_ref, qiT_ref, wiT_ref, kc_ref, vT_ref, kic_ref)
    chunks = list(range(0, tm, CONV_RC))
    share = len(chunks) // len(stages)
    for si, stage in enumerate(stages):
        stage()
        last = len(chunks) if si == len(stages) - 1 else (si + 1) * share
        for r0 in chunks[si * share:last]:
            taps(r0)

    def norm(rb, carry):
        r0 = pl.multiple_of(rb * CONV_LN_ROWS, CONV_LN_ROWS)
        acc = pre_ref[pl.ds(r0, CONV_LN_ROWS), :] + dwb_ref[...]
        mu = jnp.mean(acc, axis=-1, keepdims=True)
        xc = acc - mu
        var = jnp.mean(xc * xc, axis=-1, keepdims=True)
        y = xc * lax.rsqrt(var + NORM_EPS) * lng_ref[...] + lnb_ref[...]
        c_ref[pl.ds(r0, CONV_LN_ROWS), :] = (y * jax.nn.sigmoid(y)).astype(BF16)
        return carry

    lax.fori_loop(0, tm // CONV_LN_ROWS, norm, 0)


def _mixer_inputs(h, pos_row, wtb, wt, fq, fi, dw, dwb, lng, lnb, seq_len):
    n, d = h.shape
    cw = dw.shape[1]
    tm = MIX_TM
    assert seq_len % tm == 0 and tm % SEQ_CHUNK == 0 and CONV_HALO >= CONV_KERNEL - 1
    cpb = tm // SEQ_CHUNK
    n_chunks = n // SEQ_CHUNK
    vec = pl.BlockSpec((1, cw), lambda i: (0, 0))
    col = lambda rows: pl.BlockSpec((rows, tm), lambda i: (0, i))
    const = lambda shape: pl.BlockSpec(shape, lambda i: (0, 0), pipeline_mode=pl.Buffered(1))
    dw_rows = jnp.broadcast_to(dw[:, None, :], (CONV_KERNEL, SUBLANES, cw))
    return pl.pallas_call(
        functools.partial(_mixer_in_kernel, blocks_per_seq=seq_len // tm),
        grid=(n // tm,),
        in_specs=[pl.BlockSpec((tm, d), lambda i: (i, 0)),
                  pl.BlockSpec((1, tm), lambda i: (0, i)),
                  pl.BlockSpec((2 * cw, d), lambda i: (0, 0), pipeline_mode=pl.Buffered(1)),
                  pl.BlockSpec((_ROWS, d), lambda i: (0, 0), pipeline_mode=pl.Buffered(1)),
                  pl.BlockSpec(fq.shape, lambda i: (0, 0)),
                  pl.BlockSpec(fi.shape, lambda i: (0, 0)),
                  pl.BlockSpec((CONV_KERNEL, SUBLANES, cw), lambda i: (0, 0, 0)),
                  vec, vec, vec],
        out_specs=[pl.BlockSpec((tm, cw), lambda i: (i, 0)),
                   col(ATTN_WIDTH), col(IDX_HEADS * IDX_DIM), col(IDX_HEADS),
                   pl.BlockSpec((cpb, SEQ_CHUNK, KV_WIDTH), lambda i: (i, 0, 0)),
                   pl.BlockSpec((cpb, KV_WIDTH, SEQ_CHUNK), lambda i: (i, 0, 0)),
                   pl.BlockSpec((cpb, SEQ_CHUNK, LANES), lambda i: (i, 0, 0))],
        out_shape=[jax.ShapeDtypeStruct((n, cw), BF16),
                   jax.ShapeDtypeStruct((ATTN_WIDTH, n), BF16),
                   jax.ShapeDtypeStruct((IDX_HEADS * IDX_DIM, n), BF16),
                   jax.ShapeDtypeStruct((IDX_HEADS, n), F32),
                   jax.ShapeDtypeStruct((n_chunks, SEQ_CHUNK, KV_WIDTH), BF16),
                   jax.ShapeDtypeStruct((n_chunks, KV_WIDTH, SEQ_CHUNK), BF16),
                   jax.ShapeDtypeStruct((n_chunks, SEQ_CHUNK, LANES), BF16)],
        scratch_shapes=[pltpu.VMEM((SUBLANES, tm + CONV_HALO, cw), F32),
                        pltpu.VMEM((tm, cw), F32)],
        compiler_params=_cparams(("arbitrary",)),
        name="mixer_inputs",
    )(h, pos_row, wtb, wt, fq, fi, dw_rows, dwb, lng, lnb)


_ROW_Q = 0
_ROW_K = _ROW_Q + ATTN_WIDTH
_ROW_V = _ROW_K + KV_WIDTH
_ROW_QI = _ROW_V + KV_WIDTH
_ROW_KI = _ROW_QI + IDX_HEADS * IDX_DIM
_ROW_WI = _ROW_KI + IDX_DIM
_ROWS = _ROW_WI + IDX_HEADS


def _rope_rows(x, cos, sin, half):
    x1, x2 = x[0:half], x[half:2 * half]
    return jnp.concatenate([x1 * cos - x2 * sin, x2 * cos + x1 * sin, x[2 * half:]], axis=0)


def _proj_stages(h_ref, pos_ref, wt_ref, fq_ref, fi_ref, qT_ref, qiT_ref, wiT_ref, kc_ref, vT_ref, kic_ref):
    tm = h_ref.shape[0]
    n_chunk = tm // SEQ_CHUNK
    hq = HEAD_DIM // ROPE_FRACTION_DIV // 2
    hi = IDX_DIM // ROPE_FRACTION_DIV // 2

    def rot(f_ref):
        ang = f_ref[...] * pos_ref[...].astype(F32)
        return jnp.cos(ang), jnp.sin(ang)

    def proj(lo, hi_):
        return lax.dot_general(wt_ref[lo:hi_, :], h_ref[...], _NT, preferred_element_type=F32)

    def queries():
        cq, sq = rot(fq_ref)
        zq = proj(_ROW_Q, _ROW_K)
        for hd in range(N_HEADS):
            b = hd * HEAD_DIM
            qT_ref[b:b + HEAD_DIM, :] = _rope_rows(zq[b:b + HEAD_DIM], cq, sq, hq).astype(BF16)

    def keys_values():
        cq, sq = rot(fq_ref)
        zk = proj(_ROW_K, _ROW_V)
        kT = jnp.concatenate(
            [_rope_rows(zk[g * HEAD_DIM:(g + 1) * HEAD_DIM], cq, sq, hq) for g in range(N_KV_HEADS)], axis=0)
        kn = kT.T
        zv = proj(_ROW_V, _ROW_QI)
        for c in range(n_chunk):
            kc_ref[c] = kn[c * SEQ_CHUNK:(c + 1) * SEQ_CHUNK, :].astype(BF16)
            vT_ref[c] = zv[:, c * SEQ_CHUNK:(c + 1) * SEQ_CHUNK].astype(BF16)

    def indexer_queries():
        ci, si = rot(fi_ref)
        zqi = proj(_ROW_QI, _ROW_KI)
        for hd in range(IDX_HEADS):
            b = hd * IDX_DIM
            qiT_ref[b:b + IDX_DIM, :] = _rope_rows(zqi[b:b + IDX_DIM], ci, si, hi).astype(BF16)

    def indexer_keys():
        ci, si = rot(fi_ref)
        zr = proj(_ROW_KI, _ROWS)
        kiT = jnp.concatenate(
            [_rope_rows(zr[0:IDX_DIM], ci, si, hi), jnp.zeros((LANES - IDX_DIM, tm), F32)], axis=0)
        kin = kiT.T
        for c in range(n_chunk):
            kic_ref[c] = kin[c * SEQ_CHUNK:(c + 1) * SEQ_CHUNK, :].astype(BF16)
        wiT_ref[...] = zr[IDX_DIM:IDX_DIM + IDX_HEADS] * (IDX_HEADS ** -0.5 * IDX_DIM ** -0.5)

    return [queries, keys_values, indexer_queries, indexer_keys]


FOLD_ROWS = 4 * SUBLANES


def _fold(x):
    return x.reshape(x.shape[0] // FOLD_ROWS, FOLD_ROWS, x.shape[1])


def _dsa_kernel(qT_ref, qiT_ref, wiT_ref, kc_ref, vT_ref, kic_ref, o_ref, sc_ref, lg_ref, oacc_ref, *,
                topk, idx_steps):
    sc_len, tq = sc_ref.shape[1], sc_ref.shape[2]
    i = pl.program_id(1)
    n_vis = (i * tq) // sc_len + 1
    rep = N_HEADS // N_KV_HEADS
    neg_inf = jnp.float32(-jnp.inf)
    row = lax.broadcasted_iota(jnp.int32, (sc_len, tq), 0)
    lane = lax.broadcasted_iota(jnp.int32, (sc_len, tq), 1)
    rel = row - lane
    w = wiT_ref[...]

    def score_body(c, carry):
        mx8, mn8 = carry
        kch = kic_ref[c][:, :IDX_DIM]
        acc = jnp.zeros((sc_len, tq), F32)
        for h in range(IDX_HEADS):
            d = _dot(kch, qiT_ref[h * IDX_DIM:(h + 1) * IDX_DIM, :])
            acc = acc + w[h:h + 1, :] * jnp.maximum(d, 0.0)
        causal = rel <= i * tq - c * sc_len
        s = jnp.where(causal, acc, neg_inf)
        sc_ref[c] = s
        mx8 = jnp.maximum(mx8, _fold(s).max(axis=0))
        mn8 = jnp.minimum(mn8, _fold(jnp.where(causal, acc, jnp.inf)).min(axis=0))
        return mx8, mn8

    mx8, mn8 = lax.fori_loop(
        0, n_vis, score_body,
        (jnp.full((FOLD_ROWS, tq), -jnp.inf, F32), jnp.full((FOLD_ROWS, tq), jnp.inf, F32)))
    row_max = mx8.max(axis=0, keepdims=True)
    row_min = mn8.min(axis=0, keepdims=True)

    def count(pred):
        def body(c, cnt8):
            return cnt8 + _fold(jnp.where(pred(c, sc_ref[c]), 1.0, 0.0)).sum(axis=0)
        cnt8 = lax.fori_loop(0, n_vis, body, jnp.zeros((FOLD_ROWS, tq), F32))
        return cnt8.sum(axis=0, keepdims=True)

    def count_ge(t):
        return count(lambda c, s: s >= t)

    n_valid = (i * tq + lane[0:1, :] + 1).astype(F32)
    kk = jnp.minimum(n_valid, float(topk))
    cnt_top = count_ge(row_max)
    top_is_enough = cnt_top >= kk
    lo = jnp.where(top_is_enough, row_max, row_min)
    cnt_lo = jnp.where(top_is_enough, cnt_top, n_valid)
    hi = row_max
    cnt_hi = jnp.where(top_is_enough, 0.0, cnt_top)

    def active(lo, hi, cnt_lo):
        mid = 0.5 * lo + 0.5 * hi
        return (cnt_lo != kk) & (mid > lo) & (mid < hi)

    def any_lane(mask):
        return jnp.max(jnp.where(mask, 1, 0))

    def w_body(carry):
        lo, hi, cnt_lo, cnt_hi, _ = carry
        for _ in range(BISECT_STEPS):
            mid = 0.5 * lo + 0.5 * hi
            act = active(lo, hi, cnt_lo)
            cnt = count_ge(mid)
            up = act & (cnt >= kk)
            down = act & (cnt < kk)
            lo = jnp.where(up, mid, lo)
            cnt_lo = jnp.where(up, cnt, cnt_lo)
            hi = jnp.where(down, mid, hi)
            cnt_hi = jnp.where(down, cnt, cnt_hi)
        return lo, hi, cnt_lo, cnt_hi, any_lane(active(lo, hi, cnt_lo))

    lo, hi, cnt_lo, cnt_hi, _ = lax.while_loop(
        lambda carry: carry[4] > 0, w_body, (lo, hi, cnt_lo, cnt_hi, any_lane(active(lo, hi, cnt_lo))))
    thr = lo

    tie = cnt_lo > kk

    @pl.when(any_lane(tie) > 0)
    def _():
        need = kk - cnt_hi

        def key_index(c):
            return (c * sc_len + row).astype(F32)

        def j_body(_, carry):
            j_lo, j_hi = carry
            mid = jnp.floor(0.5 * (j_lo + j_hi))
            enough = count(lambda c, s: (s == thr) & (key_index(c) <= mid)) >= need
            return jnp.where(enough, j_lo, mid), jnp.where(enough, mid, j_hi)

        _, j_hi = lax.fori_loop(0, idx_steps, j_body, (jnp.full((1, tq), -1.0, F32), n_valid - 1.0))
        j_keep = jnp.where(tie, j_hi, jnp.inf)

        def drop_body(c, carry):
            s = sc_ref[c]
            sc_ref[c] = jnp.where((s == thr) & (key_index(c) > j_keep), neg_inf, s)
            return carry

        lax.fori_loop(0, n_vis, drop_body, 0)

    c_exp = HEAD_DIM ** -0.5 * math.log2(math.e)
    for first in range(0, N_HEADS, ATT_HEADS):
        g = first // rep
        heads = range(first, first + ATT_HEADS)

        def logits_body(c, m8s):
            kch = kc_ref[c][:, g * HEAD_DIM:(g + 1) * HEAD_DIM]
            sel = sc_ref[c] >= thr
            out = []
            for r, hd in enumerate(heads):
                l = _dot(kch, qT_ref[hd * HEAD_DIM:(hd + 1) * HEAD_DIM, :])
                l = jnp.where(sel, l, neg_inf)
                lg_ref[r, c] = l
                out.append(jnp.maximum(m8s[r], _fold(l).max(axis=0)))
            return tuple(out)

        m8s = lax.fori_loop(0, n_vis, logits_body,
                            tuple(jnp.full((FOLD_ROWS, tq), -jnp.inf, F32) for _ in heads))
        ms = [m8.max(axis=0, keepdims=True) for m8 in m8s]
        oacc_ref[...] = jnp.zeros(oacc_ref.shape, F32)

        def pv_body(c, s8s):
            vch = vT_ref[c][g * HEAD_DIM:(g + 1) * HEAD_DIM, :]
            out = []
            for r in range(ATT_HEADS):
                p = jnp.exp2((lg_ref[r, c] - ms[r]) * c_exp)
                oacc_ref[r] += _dot(vch, p.astype(BF16))
                out.append(s8s[r] + _fold(p).sum(axis=0))
            return tuple(out)

        s8s = lax.fori_loop(0, n_vis, pv_body, tuple(jnp.zeros((FOLD_ROWS, tq), F32) for _ in heads))
        for r, hd in enumerate(heads):
            den = s8s[r].sum(axis=0, keepdims=True)
            o_ref[:, hd * HEAD_DIM:(hd + 1) * HEAD_DIM] = (oacc_ref[r] / den).T.astype(BF16)


def _sparse_attention(qT, qiT, wiT, kc, vT, kic, batch, seq_len):
    n = qT.shape[1]
    tq = Q_BLOCK
    assert seq_len % SEQ_CHUNK == 0 and SEQ_CHUNK % tq == 0
    n_q = seq_len // tq
    n_c = seq_len // SEQ_CHUNK
    topk = min(TOPK_MAX, seq_len // 4)
    idx_steps = math.ceil(math.log2(seq_len)) + 1
    qcol = lambda rows: pl.BlockSpec((rows, tq), lambda b, i: (0, b * n_q + i))
    seq = lambda r, c: pl.BlockSpec((n_c, r, c), lambda b, i: (b, 0, 0))
    return pl.pallas_call(
        functools.partial(_dsa_kernel, topk=topk, idx_steps=idx_steps),
        grid=(batch, n_q),
        in_specs=[qcol(ATTN_WIDTH), qcol(IDX_HEADS * IDX_DIM), qcol(IDX_HEADS),
                  seq(SEQ_CHUNK, KV_WIDTH), seq(KV_WIDTH, SEQ_CHUNK), seq(SEQ_CHUNK, LANES)],
        out_specs=pl.BlockSpec((tq, ATTN_WIDTH), lambda b, i: (b * n_q + i, 0)),
        out_shape=jax.ShapeDtypeStruct((n, ATTN_WIDTH), BF16),
        scratch_shapes=[pltpu.VMEM((n_c, SEQ_CHUNK, tq), F32),
                        pltpu.VMEM((ATT_HEADS, n_c, SEQ_CHUNK, tq), F32),
                        pltpu.VMEM((ATT_HEADS, HEAD_DIM, tq), F32)],
        compiler_params=_cparams(("parallel", "arbitrary")),
        name="dsa_attention",
    )(qT, qiT, wiT, kc, vT, kic)


def _gate_merge_kernel(h_ref, c_ref, o_ref, wg0_ref, wg1_ref, wpw_ref, wo_ref, m_ref):
    h = h_ref[...]
    gate = lambda w_ref: jax.nn.sigmoid(
        lax.dot_general(h, w_ref[...], _NT, preferred_element_type=F32))
    g0 = gate(wg0_ref)
    g1 = gate(wg1_ref)
    y_conv = _dot(c_ref[...], wpw_ref[...].astype(BF16))
    y_attn = _dot(o_ref[...], wo_ref[...].astype(BF16))
    m_ref[...] = (g0 * y_conv + g1 * y_attn).astype(BF16)


def _gate_merge(h, c, o, wgt, wpw, wo):
    n, d = h.shape
    tm, tn = MERGE_TM, MERGE_TN
    nj = d // tn
    row = lambda width: pl.BlockSpec((tm, width), lambda j, i: (i, 0))
    colw = lambda rows: pl.BlockSpec((rows, tn), lambda j, i: (0, j))
    gate = lambda branch: pl.BlockSpec((tn, d), lambda j, i: (branch * nj + j, 0))
    return pl.pallas_call(
        _gate_merge_kernel,
        grid=(nj, n // tm),
        in_specs=[row(d), row(c.shape[1]), row(o.shape[1]),
                  gate(0), gate(1),
                  colw(wpw.shape[0]), colw(wo.shape[0])],
        out_specs=pl.BlockSpec((tm, tn), lambda j, i: (i, j)),
        out_shape=jax.ShapeDtypeStruct((n, d), BF16),
        compiler_params=_cparams(("parallel", "parallel")),
        name="gate_merge",
    )(h, c, o, wgt, wgt, wpw, wo)


def _out_proj_kernel(x_ref, m_ref, wout_ref, gpost_ref, out_ref):
    y = _dot(m_ref[...], wout_ref[...])
    out_ref[...] = x_ref[...] + _rms(y, gpost_ref[...])


def _out_proj(x, merged, wout, gpost):
    n, d = x.shape
    tm = OUT_TM
    row = pl.BlockSpec((tm, d), lambda i: (i, 0))
    return pl.pallas_call(
        _out_proj_kernel,
        grid=(n // tm,),
        in_specs=[row, row, pl.BlockSpec((d, d), lambda i: (0, 0)), pl.BlockSpec((1, d), lambda i: (0, 0))],
        out_specs=row,
        out_shape=jax.ShapeDtypeStruct((n, d), F32),
        compiler_params=_cparams(("parallel",)),
        name="out_proj",
    )(x, merged, wout, gpost)


def _cast_rows_kernel(w_ref, o_ref):
    o_ref[...] = w_ref[...].astype(BF16)


def _cast_rows(w_t, layer, n_rows):
    _, rows, d = w_t.shape
    tr = CAST_ROWS
    n_blocks = pl.cdiv(n_rows, tr)
    assert n_blocks * tr <= rows
    return pl.pallas_call(
        _cast_rows_kernel,
        grid=(n_blocks,),
        in_specs=[pl.BlockSpec((None, tr, d), lambda i: (layer, i, 0))],
        out_specs=pl.BlockSpec((tr, d), lambda i: (i, 0)),
        out_shape=jax.ShapeDtypeStruct((n_blocks * tr, d), BF16),
        compiler_params=_cparams(("parallel",)),
        name="cast_rows",
    )(w_t)


def _inv_freq(rot_dims):
    half = rot_dims // 2
    return jnp.power(jnp.float32(ROPE_THETA), -jnp.arange(half, dtype=jnp.float32) * (2.0 / rot_dims))


def kernel(x, positions, ffn1_norm_pre, ffn1_w1, ffn1_w2, ffn1_norm_post, mix_norm_pre, w_in, conv_dw, conv_dw_b, conv_ln_g, conv_ln_b, conv_w_pw, attn_w_o, w_out, mix_norm_post, ffn2_norm_pre, ffn2_w1, ffn2_w2, ffn2_norm_post):
    batch, seq_len, d = x.shape
    depth = ffn1_w1.shape[0]
    n = batch * seq_len
    xf = x.reshape(n, d)
    pos_row = positions.reshape(1, n)
    w_t = jnp.swapaxes(w_in, 1, 2)
    fq = _inv_freq(HEAD_DIM // ROPE_FRACTION_DIV).reshape(-1, 1)
    fi = _inv_freq(IDX_DIM // ROPE_FRACTION_DIV).reshape(-1, 1)

    o_q = 2 * CONV_WIDTH
    o_k = o_q + ATTN_WIDTH
    o_v = o_k + KV_WIDTH
    o_qi = o_v + KV_WIDTH
    o_ki = o_qi + IDX_HEADS * IDX_DIM
    o_wi = o_ki + IDX_DIM
    o_g = o_wi + IDX_HEADS

    for l in range(depth):
        wtb = _cast_rows(w_t, l, w_t.shape[1])
        wt = wtb[o_q:o_g]
        wgt = wtb[o_g:o_g + N_BRANCHES * d]

        xf, h = _ffn(xf, ffn1_norm_pre[l][None], ffn1_w1[l], ffn1_w2[l],
                     ffn1_norm_post[l][None], mix_norm_pre[l][None])
        c, qT, qiT, wiT, kc, vT, kic = _mixer_inputs(
            h, pos_row, wtb, wt, fq, fi, conv_dw[l], conv_dw_b[l][None], conv_ln_g[l][None], conv_ln_b[l][None],
            seq_len)
        o = _sparse_attention(qT, qiT, wiT, kc, vT, kic, batch, seq_len)
        merged = _gate_merge(h, c, o, wgt, conv_w_pw[l], attn_w_o[l])
        xf = _out_proj(xf, merged, w_out[l].astype(BF16), mix_norm_post[l][None])
        xf = _ffn(xf, ffn2_norm_pre[l][None], ffn2_w1[l], ffn2_w2[l], ffn2_norm_post[l][None])
    return xf.reshape(batch, seq_len, d)
```

```python
import functools
import math

import jax
import jax.numpy as jnp
from jax import lax
from jax.experimental import pallas as pl
from jax.experimental.pallas import tpu as pltpu

F32 = jnp.float32
BF16 = jnp.bfloat16

CONV_WIDTH = 1024
CONV_KERNEL = 31
N_HEADS = 8
N_KV_HEADS = 2
N_BRANCHES = 2
HEAD_DIM = 128
ATTN_WIDTH = N_HEADS * HEAD_DIM
KV_WIDTH = N_KV_HEADS * HEAD_DIM
IDX_HEADS = 16
IDX_DIM = 64
TOPK_MAX = 256
ROPE_THETA = 500000.0
ROPE_FRACTION_DIV = 4
NORM_EPS = 1e-6

V7X_VMEM_BYTES = 64 * 1024 * 1024
VMEM_LIMIT_BYTES = V7X_VMEM_BYTES - 3 * 1024 * 1024

SUBLANES = 8
LANES = 128

FFN_TM = 1024
FFN_TF = 256
MIX_TM = 512
CONV_RC = 32
CONV_LW = 512
CONV_HALO = 32
CONV_LN_ROWS = 128
SEQ_CHUNK = 512
Q_BLOCK = 512
ATT_HEADS = 2
BISECT_STEPS = 4
MERGE_TM = 1024
MERGE_TN = 512
OUT_TM = 512
CAST_ROWS = 976

_NT = (((1,), (1,)), ((), ()))


def _cparams(sem):
    return pltpu.CompilerParams(dimension_semantics=sem, vmem_limit_bytes=VMEM_LIMIT_BYTES)


def _rms(x, gain):
    ms = jnp.mean(x * x, axis=-1, keepdims=True)
    return x * lax.rsqrt(ms + NORM_EPS) * gain


def _dot(a, b):
    return jnp.dot(a, b, preferred_element_type=F32)


def _ffn_kernel(*refs, nj, emit_h):
    if emit_h:
        x_ref, gpre_ref, w1g_ref, w1u_ref, w2_ref, gpost_ref, gnext_ref, o_ref, h_ref, hn_ref = refs
    else:
        x_ref, gpre_ref, w1g_ref, w1u_ref, w2_ref, gpost_ref, o_ref, hn_ref = refs
    j = pl.program_id(1)

    @pl.when(j == 0)
    def _():
        hn_ref[...] = _rms(x_ref[...], gpre_ref[...]).astype(BF16)
        o_ref[...] = jnp.zeros(o_ref.shape, F32)

    hn = hn_ref[...]
    g = _dot(hn, w1g_ref[...].astype(BF16))
    u = _dot(hn, w1u_ref[...].astype(BF16))
    a = (g * jax.nn.sigmoid(g) * u).astype(BF16)
    o_ref[...] += _dot(a, w2_ref[...].astype(BF16))

    @pl.when(j == nj - 1)
    def _():
        out = x_ref[...] + 0.5 * _rms(o_ref[...], gpost_ref[...])
        o_ref[...] = out
        if emit_h:
            h_ref[...] = _rms(out, gnext_ref[...]).astype(BF16)


def _ffn(x, gpre, w1, w2, gpost, gnext=None):
    n, d = x.shape
    f = w2.shape[0]
    tm, tf = FFN_TM, FFN_TF
    nj = f // tf
    assert n % tm == 0 and f % tf == 0
    emit_h = gnext is not None
    row = pl.BlockSpec((tm, d), lambda i, j: (i, 0))
    vec = pl.BlockSpec((1, d), lambda i, j: (0, 0))
    in_specs = [row, vec,
                pl.BlockSpec((d, tf), lambda i, j: (0, j)),
                pl.BlockSpec((d, tf), lambda i, j: (0, j + nj)),
                pl.BlockSpec((tf, d), lambda i, j: (j, 0)),
                vec]
    args = [x, gpre, w1, w1, w2, gpost]
    out_shape = [jax.ShapeDtypeStruct((n, d), F32)]
    out_specs = [row]
    if emit_h:
        in_specs.append(vec)
        args.append(gnext)
        out_shape.append(jax.ShapeDtypeStruct((n, d), BF16))
        out_specs.append(row)
    res = pl.pallas_call(
        functools.partial(_ffn_kernel, nj=nj, emit_h=emit_h),
        grid=(n // tm, nj),
        in_specs=in_specs,
        out_specs=out_specs,
        out_shape=out_shape,
        scratch_shapes=[pltpu.VMEM((tm, d), BF16)],
        compiler_params=_cparams(("parallel", "arbitrary")),
        name="ffn_emit_h" if emit_h else "ffn",
    )(*args)
    return res if emit_h else res[0]


def _mixer_in_kernel(h_ref, pos_ref, wab_ref, wt_ref, fq_ref, fi_ref, dw_ref, dwb_ref, lng_ref, lnb_ref,
                     c_ref, qT_ref, qiT_ref, wiT_ref, kc_ref, vT_ref, kic_ref, sh_ref, pre_ref, *, blocks_per_seq):
    tm = h_ref.shape[0]
    cw = c_ref.shape[1]
    halo = CONV_HALO
    i = pl.program_id(0)
    first = (i % blocks_per_seq) == 0

    @pl.when(first)
    def _():
        sh_ref[0, 0:halo, :] = jnp.zeros((halo, cw), F32)

    @pl.when(jnp.logical_not(first))
    def _():
        sh_ref[0, 0:halo, :] = sh_ref[0, tm:tm + halo, :]

    z = lax.dot_general(h_ref[...], wab_ref[...], _NT, preferred_element_type=F32)
    sh_ref[0, halo:halo + tm, :] = z[:, :cw] * jax.nn.sigmoid(z[:, cw:])

    span = tm + halo - SUBLANES
    for s in range(1, SUBLANES):
        sh_ref[s, 0:span, :] = sh_ref[0, s:s + span, :]

    first_tap = halo - (CONV_KERNEL - 1)

    def taps(r0):
        for lt in range(cw // CONV_LW):
            lanes = slice(lt * CONV_LW, (lt + 1) * CONV_LW)
            acc = jnp.zeros((CONV_RC // SUBLANES, SUBLANES, CONV_LW), F32)
            for k in range(CONV_KERNEL):
                off = first_tap + k
                s, base = off % SUBLANES, off - off % SUBLANES
                x = sh_ref[s, r0 + base:r0 + base + CONV_RC, lanes]
                acc = acc + dw_ref[k, :, lanes][None] * x.reshape(acc.shape)
            pre_ref[r0:r0 + CONV_RC, lanes] = acc.reshape(CONV_RC, CONV_LW)

    stages = _proj_stages(h_ref, pos_ref, wt_ref, fq_ref, fi_ref, qT_ref, qiT_ref, wiT_ref, kc_ref, vT_ref, kic_ref)
    chunks = list(range(0, tm, CONV_RC))
    share = len(chunks) // len(stages)
    for si, stage in enumerate(stages):
        stage()
        last = len(chunks) if si == len(stages) - 1 else (si + 1) * share
        for r0 in chunks[si * share:last]:
            taps(r0)

    def norm(rb, carry):
        r0 = pl.multiple_of(rb * CONV_LN_ROWS, CONV_LN_ROWS)
        acc = pre_ref[pl.ds(r0, CONV_LN_ROWS), :] + dwb_ref[...]
        mu = jnp.mean(acc, axis=-1, keepdims=True)
        xc = acc - mu
        var = jnp.mean(xc * xc, axis=-1, keepdims=True)
        y = xc * lax.rsqrt(var + NORM_EPS) * lng_ref[...] + lnb_ref[...]
        c_ref[pl.ds(r0, CONV_LN_ROWS), :] = (y * jax.nn.sigmoid(y)).astype(BF16)
        return carry

    lax.fori_loop(0, tm // CONV_LN_ROWS, norm, 0)


def _mixer_inputs(h, pos_row, wtb, fq, fi, dw, dwb, lng, lnb, seq_len):
    n, d = h.shape
    cw = dw.shape[1]
    tm = MIX_TM
    assert seq_len % tm == 0 and tm % SEQ_CHUNK == 0 and CONV_HALO >= CONV_KERNEL - 1
    cpb = tm // SEQ_CHUNK
    n_chunks = n // SEQ_CHUNK
    vec = pl.BlockSpec((1, cw), lambda i: (0, 0))
    col = lambda rows: pl.BlockSpec((rows, tm), lambda i: (0, i))
    const = lambda shape: pl.BlockSpec(shape, lambda i: (0, 0), pipeline_mode=pl.Buffered(1))
    dw_rows = jnp.broadcast_to(dw[:, None, :], (CONV_KERNEL, SUBLANES, cw))
    return pl.pallas_call(
        functools.partial(_mixer_in_kernel, blocks_per_seq=seq_len // tm),
        grid=(n // tm,),
        in_specs=[pl.BlockSpec((tm, d), lambda i: (i, 0)),
                  pl.BlockSpec((1, tm), lambda i: (0, i)),
                  pl.BlockSpec((2 * cw, d), lambda i: (0, 0), pipeline_mode=pl.Buffered(1)),
                  pl.BlockSpec((pl.Element(_ROWS), pl.Element(d)),
                               lambda i: (pl.multiple_of(i * 0 + 2 * cw, 2 * SUBLANES), 0),
                               pipeline_mode=pl.Buffered(1)),
                  pl.BlockSpec(fq.shape, lambda i: (0, 0)),
                  pl.BlockSpec(fi.shape, lambda i: (0, 0)),
                  pl.BlockSpec((CONV_KERNEL, SUBLANES, cw), lambda i: (0, 0, 0)),
                  vec, vec, vec],
        out_specs=[pl.BlockSpec((tm, cw), lambda i: (i, 0)),
                   col(ATTN_WIDTH), col(IDX_HEADS * IDX_DIM), col(IDX_HEADS),
                   pl.BlockSpec((cpb, SEQ_CHUNK, KV_WIDTH), lambda i: (i, 0, 0)),
                   pl.BlockSpec((cpb, KV_WIDTH, SEQ_CHUNK), lambda i: (i, 0, 0)),
                   pl.BlockSpec((cpb, SEQ_CHUNK, LANES), lambda i: (i, 0, 0))],
        out_shape=[jax.ShapeDtypeStruct((n, cw), BF16),
                   jax.ShapeDtypeStruct((ATTN_WIDTH, n), BF16),
                   jax.ShapeDtypeStruct((IDX_HEADS * IDX_DIM, n), BF16),
                   jax.ShapeDtypeStruct((IDX_HEADS, n), F32),
                   jax.ShapeDtypeStruct((n_chunks, SEQ_CHUNK, KV_WIDTH), BF16),
                   jax.ShapeDtypeStruct((n_chunks, KV_WIDTH, SEQ_CHUNK), BF16),
                   jax.ShapeDtypeStruct((n_chunks, SEQ_CHUNK, LANES), BF16)],
        scratch_shapes=[pltpu.VMEM((SUBLANES, tm + CONV_HALO, cw), F32),
                        pltpu.VMEM((tm, cw), F32)],
        compiler_params=_cparams(("arbitrary",)),
        name="mixer_inputs",
    )(h, pos_row, wtb, wtb, fq, fi, dw_rows, dwb, lng, lnb)


_ROW_Q = 0
_ROW_K = _ROW_Q + ATTN_WIDTH
_ROW_V = _ROW_K + KV_WIDTH
_ROW_QI = _ROW_V + KV_WIDTH
_ROW_KI = _ROW_QI + IDX_HEADS * IDX_DIM
_ROW_WI = _ROW_KI + IDX_DIM
_ROWS = _ROW_WI + IDX_HEADS


def _rope_rows(x, cos, sin, half):
    x1, x2 = x[0:half], x[half:2 * half]
    return jnp.concatenate([x1 * cos - x2 * sin, x2 * cos + x1 * sin, x[2 * half:]], axis=0)


def _proj_stages(h_ref, pos_ref, wt_ref, fq_ref, fi_ref, qT_ref, qiT_ref, wiT_ref, kc_ref, vT_ref, kic_ref):
    tm = h_ref.shape[0]
    n_chunk = tm // SEQ_CHUNK
    hq = HEAD_DIM // ROPE_FRACTION_DIV // 2
    hi = IDX_DIM // ROPE_FRACTION_DIV // 2

    def rot(f_ref):
        ang = f_ref[...] * pos_ref[...].astype(F32)
        return jnp.cos(ang), jnp.sin(ang)

    def proj(lo, hi_):
        return lax.dot_general(wt_ref[lo:hi_, :], h_ref[...], _NT, preferred_element_type=F32)

    def queries():
        cq, sq = rot(fq_ref)
        zq = proj(_ROW_Q, _ROW_K)
        for hd in range(N_HEADS):
            b = hd * HEAD_DIM
            qT_ref[b:b + HEAD_DIM, :] = _rope_rows(zq[b:b + HEAD_DIM], cq, sq, hq).astype(BF16)

    def keys_values():
        cq, sq = rot(fq_ref)
        zk = proj(_ROW_K, _ROW_V)
        kT = jnp.concatenate(
            [_rope_rows(zk[g * HEAD_DIM:(g + 1) * HEAD_DIM], cq, sq, hq) for g in range(N_KV_HEADS)], axis=0)
        kn = kT.T
        zv = proj(_ROW_V, _ROW_QI)
        for c in range(n_chunk):
            kc_ref[c] = kn[c * SEQ_CHUNK:(c + 1) * SEQ_CHUNK, :].astype(BF16)
            vT_ref[c] = zv[:, c * SEQ_CHUNK:(c + 1) * SEQ_CHUNK].astype(BF16)

    def indexer_queries():
        ci, si = rot(fi_ref)
        zqi = proj(_ROW_QI, _ROW_KI)
        for hd in range(IDX_HEADS):
            b = hd * IDX_DIM
            qiT_ref[b:b + IDX_DIM, :] = _rope_rows(zqi[b:b + IDX_DIM], ci, si, hi).astype(BF16)

    def indexer_keys():
        ci, si = rot(fi_ref)
        zr = proj(_ROW_KI, _ROWS)
        kiT = jnp.concatenate(
            [_rope_rows(zr[0:IDX_DIM], ci, si, hi), jnp.zeros((LANES - IDX_DIM, tm), F32)], axis=0)
        kin = kiT.T
        for c in range(n_chunk):
            kic_ref[c] = kin[c * SEQ_CHUNK:(c + 1) * SEQ_CHUNK, :].astype(BF16)
        wiT_ref[...] = zr[IDX_DIM:IDX_DIM + IDX_HEADS] * (IDX_HEADS ** -0.5 * IDX_DIM ** -0.5)

    return [queries, keys_values, indexer_queries, indexer_keys]


FOLD_ROWS = 4 * SUBLANES


def _fold(x):
    return x.reshape(x.shape[0] // FOLD_ROWS, FOLD_ROWS, x.shape[1])


def _dsa_kernel(qT_ref, qiT_ref, wiT_ref, kc_ref, vT_ref, kic_ref, o_ref, sc_ref, lg_ref, oacc_ref, *,
                topk, idx_steps):
    sc_len, tq = sc_ref.shape[1], sc_ref.shape[2]
    i = pl.program_id(1)
    n_vis = (i * tq) // sc_len + 1
    rep = N_HEADS // N_KV_HEADS
    neg_inf = jnp.float32(-jnp.inf)
    row = lax.broadcasted_iota(jnp.int32, (sc_len, tq), 0)
    lane = lax.broadcasted_iota(jnp.int32, (sc_len, tq), 1)
    rel = row - lane
    w = wiT_ref[...]

    def score_body(c, carry):
        mx8, mn8 = carry
        kch = kic_ref[c][:, :IDX_DIM]
        acc = jnp.zeros((sc_len, tq), F32)
        for h in range(IDX_HEADS):
            d = _dot(kch, qiT_ref[h * IDX_DIM:(h + 1) * IDX_DIM, :])
            acc = acc + w[h:h + 1, :] * jnp.maximum(d, 0.0)
        causal = rel <= i * tq - c * sc_len
        s = jnp.where(causal, acc, neg_inf)
        sc_ref[c] = s
        mx8 = jnp.maximum(mx8, _fold(s).max(axis=0))
        mn8 = jnp.minimum(mn8, _fold(jnp.where(causal, acc, jnp.inf)).min(axis=0))
        return mx8, mn8

    mx8, mn8 = lax.fori_loop(
        0, n_vis, score_body,
        (jnp.full((FOLD_ROWS, tq), -jnp.inf, F32), jnp.full((FOLD_ROWS, tq), jnp.inf, F32)))
    row_max = mx8.max(axis=0, keepdims=True)
    row_min = mn8.min(axis=0, keepdims=True)

    def count(pred):
        def body(c, cnt8):
            return cnt8 + _fold(jnp.where(pred(c, sc_ref[c]), 1.0, 0.0)).sum(axis=0)
        cnt8 = lax.fori_loop(0, n_vis, body, jnp.zeros((FOLD_ROWS, tq), F32))
        return cnt8.sum(axis=0, keepdims=True)

    def count_ge(t):
        return count(lambda c, s: s >= t)

    n_valid = (i * tq + lane[0:1, :] + 1).astype(F32)
    kk = jnp.minimum(n_valid, float(topk))
    cnt_top = count_ge(row_max)
    top_is_enough = cnt_top >= kk
    lo = jnp.where(top_is_enough, row_max, row_min)
    cnt_lo = jnp.where(top_is_enough, cnt_top, n_valid)
    hi = row_max
    cnt_hi = jnp.where(top_is_enough, 0.0, cnt_top)

    def active(lo, hi, cnt_lo):
        mid = 0.5 * lo + 0.5 * hi
        return (cnt_lo != kk) & (mid > lo) & (mid < hi)

    def any_lane(mask):
        return jnp.max(jnp.where(mask, 1, 0))

    def w_body(carry):
        lo, hi, cnt_lo, cnt_hi, _ = carry
        for _ in range(BISECT_STEPS):
            mid = 0.5 * lo + 0.5 * hi
            act = active(lo, hi, cnt_lo)
            cnt = count_ge(mid)
            up = act & (cnt >= kk)
            down = act & (cnt < kk)
            lo = jnp.where(up, mid, lo)
            cnt_lo = jnp.where(up, cnt, cnt_lo)
            hi = jnp.where(down, mid, hi)
            cnt_hi = jnp.where(down, cnt, cnt_hi)
        return lo, hi, cnt_lo, cnt_hi, any_lane(active(lo, hi, cnt_lo))

    lo, hi, cnt_lo, cnt_hi, _ = lax.while_loop(
        lambda carry: carry[4] > 0, w_body, (lo, hi, cnt_lo, cnt_hi, any_lane(active(lo, hi, cnt_lo))))
    thr = lo

    tie = cnt_lo > kk

    @pl.when(any_lane(tie) > 0)
    def _():
        need = kk - cnt_hi

        def key_index(c):
            return (c * sc_len + row).astype(F32)

        def j_body(_, carry):
            j_lo, j_hi = carry
            mid = jnp.floor(0.5 * (j_lo + j_hi))
            enough = count(lambda c, s: (s == thr) & (key_index(c) <= mid)) >= need
            return jnp.where(enough, j_lo, mid), jnp.where(enough, mid, j_hi)

        _, j_hi = lax.fori_loop(0, idx_steps, j_body, (jnp.full((1, tq), -1.0, F32), n_valid - 1.0))
        j_keep = jnp.where(tie, j_hi, jnp.inf)

        def drop_body(c, carry):
            s = sc_ref[c]
            sc_ref[c] = jnp.where((s == thr) & (key_index(c) > j_keep), neg_inf, s)
            return carry

        lax.fori_loop(0, n_vis, drop_body, 0)

    c_exp = HEAD_DIM ** -0.5 * math.log2(math.e)
    for first in range(0, N_HEADS, ATT_HEADS):
        g = first // rep
        heads = range(first, first + ATT_HEADS)

        def logits_body(c, m8s):
            kch = kc_ref[c][:, g * HEAD_DIM:(g + 1) * HEAD_DIM]
            sel = sc_ref[c] >= thr
            out = []
            for r, hd in enumerate(heads):
                l = _dot(kch, qT_ref[hd * HEAD_DIM:(hd + 1) * HEAD_DIM, :])
                l = jnp.where(sel, l, neg_inf)
                lg_ref[r, c] = l
                out.append(jnp.maximum(m8s[r], _fold(l).max(axis=0)))
            return tuple(out)

        m8s = lax.fori_loop(0, n_vis, logits_body,
                            tuple(jnp.full((FOLD_ROWS, tq), -jnp.inf, F32) for _ in heads))
        ms = [m8.max(axis=0, keepdims=True) for m8 in m8s]
        oacc_ref[...] = jnp.zeros(oacc_ref.shape, F32)

        def pv_body(c, s8s):
            vch = vT_ref[c][g * HEAD_DIM:(g + 1) * HEAD_DIM, :]
            out = []
            for r in range(ATT_HEADS):
                p = jnp.exp2((lg_ref[r, c] - ms[r]) * c_exp)
                oacc_ref[r] += _dot(vch, p.astype(BF16))
                out.append(s8s[r] + _fold(p).sum(axis=0))
            return tuple(out)

        s8s = lax.fori_loop(0, n_vis, pv_body, tuple(jnp.zeros((FOLD_ROWS, tq), F32) for _ in heads))
        for r, hd in enumerate(heads):
            den = s8s[r].sum(axis=0, keepdims=True)
            o_ref[:, hd * HEAD_DIM:(hd + 1) * HEAD_DIM] = (oacc_ref[r] / den).T.astype(BF16)


def _sparse_attention(qT, qiT, wiT, kc, vT, kic, batch, seq_len):
    n = qT.shape[1]
    tq = Q_BLOCK
    assert seq_len % SEQ_CHUNK == 0 and SEQ_CHUNK % tq == 0
    n_q = seq_len // tq
    n_c = seq_len // SEQ_CHUNK
    topk = min(TOPK_MAX, seq_len // 4)
    idx_steps = math.ceil(math.log2(seq_len)) + 1
    qcol = lambda rows: pl.BlockSpec((rows, tq), lambda b, i: (0, b * n_q + i))
    seq = lambda r, c: pl.BlockSpec((n_c, r, c), lambda b, i: (b, 0, 0))
    return pl.pallas_call(
        functools.partial(_dsa_kernel, topk=topk, idx_steps=idx_steps),
        grid=(batch, n_q),
        in_specs=[qcol(ATTN_WIDTH), qcol(IDX_HEADS * IDX_DIM), qcol(IDX_HEADS),
                  seq(SEQ_CHUNK, KV_WIDTH), seq(KV_WIDTH, SEQ_CHUNK), seq(SEQ_CHUNK, LANES)],
        out_specs=pl.BlockSpec((tq, ATTN_WIDTH), lambda b, i: (b * n_q + i, 0)),
        out_shape=jax.ShapeDtypeStruct((n, ATTN_WIDTH), BF16),
        scratch_shapes=[pltpu.VMEM((n_c, SEQ_CHUNK, tq), F32),
                        pltpu.VMEM((ATT_HEADS, n_c, SEQ_CHUNK, tq), F32),
                        pltpu.VMEM((ATT_HEADS, HEAD_DIM, tq), F32)],
        compiler_params=_cparams(("parallel", "arbitrary")),
        name="dsa_attention",
    )(qT, qiT, wiT, kc, vT, kic)


def _gate_merge_kernel(h_ref, c_ref, o_ref, wg0_ref, wg1_ref, wpw_ref, wo_ref, m_ref):
    h = h_ref[...]
    gate = lambda w_ref: jax.nn.sigmoid(
        lax.dot_general(h, w_ref[...], _NT, preferred_element_type=F32))
    g0 = gate(wg0_ref)
    g1 = gate(wg1_ref)
    y_conv = _dot(c_ref[...], wpw_ref[...].astype(BF16))
    y_attn = _dot(o_ref[...], wo_ref[...].astype(BF16))
    m_ref[...] = (g0 * y_conv + g1 * y_attn).astype(BF16)


def _gate_merge(h, c, o, wtb, gate_row, wpw, wo):
    assert gate_row % (2 * SUBLANES) == 0
    n, d = h.shape
    tm, tn = MERGE_TM, MERGE_TN
    nj = d // tn
    row = lambda width: pl.BlockSpec((tm, width), lambda j, i: (i, 0))
    colw = lambda rows: pl.BlockSpec((rows, tn), lambda j, i: (0, j))
    gate = lambda branch: pl.BlockSpec(
        (pl.Element(tn), pl.Element(d)),
        lambda j, i: (pl.multiple_of(gate_row + branch * d + j * tn, 2 * SUBLANES), 0))
    return pl.pallas_call(
        _gate_merge_kernel,
        grid=(nj, n // tm),
        in_specs=[row(d), row(c.shape[1]), row(o.shape[1]),
                  gate(0), gate(1),
                  colw(wpw.shape[0]), colw(wo.shape[0])],
        out_specs=pl.BlockSpec((tm, tn), lambda j, i: (i, j)),
        out_shape=jax.ShapeDtypeStruct((n, d), BF16),
        compiler_params=_cparams(("parallel", "parallel")),
        name="gate_merge",
    )(h, c, o, wtb, wtb, wpw, wo)


def _out_proj_kernel(x_ref, m_ref, wout_ref, gpost_ref, out_ref):
    y = _dot(m_ref[...], wout_ref[...].astype(BF16))
    out_ref[...] = x_ref[...] + _rms(y, gpost_ref[...])


def _out_proj(x, merged, wout, gpost):
    n, d = x.shape
    tm = OUT_TM
    row = pl.BlockSpec((tm, d), lambda i: (i, 0))
    return pl.pallas_call(
        _out_proj_kernel,
        grid=(n // tm,),
        in_specs=[row, row,
                  pl.BlockSpec((d, d), lambda i: (0, 0), pipeline_mode=pl.Buffered(1)),
                  pl.BlockSpec((1, d), lambda i: (0, 0))],
        out_specs=row,
        out_shape=jax.ShapeDtypeStruct((n, d), F32),
        compiler_params=_cparams(("parallel",)),
        name="out_proj",
    )(x, merged, wout, gpost)


def _cast_rows_kernel(w_ref, o_ref):
    o_ref[...] = w_ref[...].astype(BF16)


def _cast_rows(w_t, layer, n_rows):
    _, rows, d = w_t.shape
    tr = CAST_ROWS
    n_blocks = pl.cdiv(n_rows, tr)
    assert n_blocks * tr <= rows
    return pl.pallas_call(
        _cast_rows_kernel,
        grid=(n_blocks,),
        in_specs=[pl.BlockSpec((None, tr, d), lambda i: (layer, i, 0))],
        out_specs=pl.BlockSpec((tr, d), lambda i: (i, 0)),
        out_shape=jax.ShapeDtypeStruct((n_blocks * tr, d), BF16),
        compiler_params=_cparams(("parallel",)),
        name="cast_rows",
    )(w_t)


def _inv_freq(rot_dims):
    half = rot_dims // 2
    return jnp.power(jnp.float32(ROPE_THETA), -jnp.arange(half, dtype=jnp.float32) * (2.0 / rot_dims))


def kernel(x, positions, ffn1_norm_pre, ffn1_w1, ffn1_w2, ffn1_norm_post, mix_norm_pre, w_in, conv_dw, conv_dw_b, conv_ln_g, conv_ln_b, conv_w_pw, attn_w_o, w_out, mix_norm_post, ffn2_norm_pre, ffn2_w1, ffn2_w2, ffn2_norm_post):
    batch, seq_len, d = x.shape
    depth = ffn1_w1.shape[0]
    n = batch * seq_len
    xf = x.reshape(n, d)
    pos_row = positions.reshape(1, n)
    w_t = jnp.swapaxes(w_in, 1, 2)
    fq = _inv_freq(HEAD_DIM // ROPE_FRACTION_DIV).reshape(-1, 1)
    fi = _inv_freq(IDX_DIM // ROPE_FRACTION_DIV).reshape(-1, 1)

    o_q = 2 * CONV_WIDTH
    o_k = o_q + ATTN_WIDTH
    o_v = o_k + KV_WIDTH
    o_qi = o_v + KV_WIDTH
    o_ki = o_qi + IDX_HEADS * IDX_DIM
    o_wi = o_ki + IDX_DIM
    o_g = o_wi + IDX_HEADS

    for l in range(depth):
        wtb = _cast_rows(w_t, l, w_t.shape[1])

        xf, h = _ffn(xf, ffn1_norm_pre[l][None], ffn1_w1[l], ffn1_w2[l],
                     ffn1_norm_post[l][None], mix_norm_pre[l][None])
        c, qT, qiT, wiT, kc, vT, kic = _mixer_inputs(
            h, pos_row, wtb, fq, fi, conv_dw[l], conv_dw_b[l][None], conv_ln_g[l][None], conv_ln_b[l][None],
            seq_len)
        o = _sparse_attention(qT, qiT, wiT, kc, vT, kic, batch, seq_len)
        merged = _gate_merge(h, c, o, wtb, o_g, conv_w_pw[l], attn_w_o[l])
        xf = _out_proj(xf, merged, w_out[l], mix_norm_post[l][None])
        xf = _ffn(xf, ffn2_norm_pre[l][None], ffn2_w1[l], ffn2_w2[l], ffn2_norm_post[l][None])
    return xf.reshape(batch, seq_len, d)
```

```python
import functools
import math

import jax
import jax.numpy as jnp
from jax import lax
from jax.experimental import pallas as pl
from jax.experimental.pallas import tpu as pltpu

F32 = jnp.float32
BF16 = jnp.bfloat16

CONV_WIDTH = 1024
CONV_KERNEL = 31
N_HEADS = 8
N_KV_HEADS = 2
N_BRANCHES = 2
HEAD_DIM = 128
ATTN_WIDTH = N_HEADS * HEAD_DIM
KV_WIDTH = N_KV_HEADS * HEAD_DIM
IDX_HEADS = 16
IDX_DIM = 64
TOPK_MAX = 256
ROPE_THETA = 500000.0
ROPE_FRACTION_DIV = 4
NORM_EPS = 1e-6

V7X_VMEM_BYTES = 64 * 1024 * 1024
VMEM_LIMIT_BYTES = V7X_VMEM_BYTES - 3 * 1024 * 1024

SUBLANES = 8
LANES = 128

FFN_TM = 1024
FFN_TF = 256
MIX_TM = 512
CONV_RC = 32
CONV_LW = 512
CONV_HALO = 32
CONV_LN_ROWS = 128
SEQ_CHUNK = 512
Q_BLOCK = 512
ATT_HEADS = 2
BISECT_STEPS = 4
MERGE_TM = 1024
MERGE_TN = 512
OUT_TM = 512
CAST_ROWS = 976

_NT = (((1,), (1,)), ((), ()))


def _cparams(sem):
    return pltpu.CompilerParams(dimension_semantics=sem, vmem_limit_bytes=VMEM_LIMIT_BYTES)


def _rms(x, gain):
    ms = jnp.mean(x * x, axis=-1, keepdims=True)
    return x * lax.rsqrt(ms + NORM_EPS) * gain


def _dot(a, b):
    return jnp.dot(a, b, preferred_element_type=F32)


def _ffn_kernel(*refs, nj, emit_h):
    if emit_h:
        x_ref, gpre_ref, w1g_ref, w1u_ref, w2_ref, gpost_ref, gnext_ref, o_ref, h_ref, hn_ref = refs
    else:
        x_ref, gpre_ref, w1g_ref, w1u_ref, w2_ref, gpost_ref, o_ref, hn_ref = refs
    j = pl.program_id(1)

    @pl.when(j == 0)
    def _():
        hn_ref[...] = _rms(x_ref[...], gpre_ref[...]).astype(BF16)
        o_ref[...] = jnp.zeros(o_ref.shape, F32)

    hn = hn_ref[...]
    g = _dot(hn, w1g_ref[...].astype(BF16))
    u = _dot(hn, w1u_ref[...].astype(BF16))
    a = (g * jax.nn.sigmoid(g) * u).astype(BF16)
    o_ref[...] += _dot(a, w2_ref[...].astype(BF16))

    @pl.when(j == nj - 1)
    def _():
        out = x_ref[...] + 0.5 * _rms(o_ref[...], gpost_ref[...])
        o_ref[...] = out
        if emit_h:
            h_ref[...] = _rms(out, gnext_ref[...]).astype(BF16)


def _ffn(x, gpre, w1, w2, gpost, gnext=None):
    n, d = x.shape
    f = w2.shape[0]
    tm, tf = FFN_TM, FFN_TF
    nj = f // tf
    assert n % tm == 0 and f % tf == 0
    emit_h = gnext is not None
    row = pl.BlockSpec((tm, d), lambda i, j: (i, 0))
    vec = pl.BlockSpec((1, d), lambda i, j: (0, 0))
    in_specs = [row, vec,
                pl.BlockSpec((d, tf), lambda i, j: (0, j)),
                pl.BlockSpec((d, tf), lambda i, j: (0, j + nj)),
                pl.BlockSpec((tf, d), lambda i, j: (j, 0)),
                vec]
    args = [x, gpre, w1, w1, w2, gpost]
    out_shape = [jax.ShapeDtypeStruct((n, d), F32)]
    out_specs = [row]
    if emit_h:
        in_specs.append(vec)
        args.append(gnext)
        out_shape.append(jax.ShapeDtypeStruct((n, d), BF16))
        out_specs.append(row)
    res = pl.pallas_call(
        functools.partial(_ffn_kernel, nj=nj, emit_h=emit_h),
        grid=(n // tm, nj),
        in_specs=in_specs,
        out_specs=out_specs,
        out_shape=out_shape,
        scratch_shapes=[pltpu.VMEM((tm, d), BF16)],
        compiler_params=_cparams(("parallel", "arbitrary")),
        name="ffn_emit_h" if emit_h else "ffn",
    )(*args)
    return res if emit_h else res[0]


def _mixer_in_kernel(h_ref, pos_ref, wab_ref, wt_ref, fq_ref, fi_ref, dw_ref, dwb_ref, lng_ref, lnb_ref,
                     c_ref, qT_ref, qiT_ref, wiT_ref, kc_ref, vT_ref, kic_ref, sh_ref, pre_ref, *, blocks_per_seq):
    tm = h_ref.shape[0]
    cw = c_ref.shape[1]
    halo = CONV_HALO
    i = pl.program_id(0)
    first = (i % blocks_per_seq) == 0

    @pl.when(first)
    def _():
        sh_ref[0, 0:halo, :] = jnp.zeros((halo, cw), F32)

    @pl.when(jnp.logical_not(first))
    def _():
        sh_ref[0, 0:halo, :] = sh_ref[0, tm:tm + halo, :]

    z = lax.dot_general(h_ref[...], wab_ref[...], _NT, preferred_element_type=F32)
    sh_ref[0, halo:halo + tm, :] = z[:, :cw] * jax.nn.sigmoid(z[:, cw:])

    span = tm + halo - SUBLANES
    for s in range(1, SUBLANES):
        sh_ref[s, 0:span, :] = sh_ref[0, s:s + span, :]

    first_tap = halo - (CONV_KERNEL - 1)

    def taps(r0):
        for lt in range(cw // CONV_LW):
            lanes = slice(lt * CONV_LW, (lt + 1) * CONV_LW)
            acc = jnp.zeros((CONV_RC // SUBLANES, SUBLANES, CONV_LW), F32)
            for k in range(CONV_KERNEL):
                off = first_tap + k
                s, base = off % SUBLANES, off - off % SUBLANES
                x = sh_ref[s, r0 + base:r0 + base + CONV_RC, lanes]
                acc = acc + dw_ref[k, :, lanes][None] * x.reshape(acc.shape)
            pre_ref[r0:r0 + CONV_RC, lanes] = acc.reshape(CONV_RC, CONV_LW)

    stages = _proj_stages(h_ref, pos_ref, wt_ref, fq_ref, fi_ref, qT_ref, qiT_ref, wiT_ref, kc_ref, vT_ref, kic_ref)
    chunks = list(range(0, tm, CONV_RC))
    share = len(chunks) // len(stages)
    for si, stage in enumerate(stages):
        stage()
        last = len(chunks) if si == len(stages) - 1 else (si + 1) * share
        for r0 in chunks[si * share:last]:
            taps(r0)

    def norm(rb, carry):
        r0 = pl.multiple_of(rb * CONV_LN_ROWS, CONV_LN_ROWS)
        acc = pre_ref[pl.ds(r0, CONV_LN_ROWS), :] + dwb_ref[...]
        mu = jnp.mean(acc, axis=-1, keepdims=True)
        xc = acc - mu
        var = jnp.mean(xc * xc, axis=-1, keepdims=True)
        y = xc * lax.rsqrt(var + NORM_EPS) * lng_ref[...] + lnb_ref[...]
        c_ref[pl.ds(r0, CONV_LN_ROWS), :] = (y * jax.nn.sigmoid(y)).astype(BF16)
        return carry

    lax.fori_loop(0, tm // CONV_LN_ROWS, norm, 0)


def _mixer_inputs(h, pos_row, wtb, fq, fi, dw, dwb, lng, lnb, seq_len):
    n, d = h.shape
    cw = dw.shape[1]
    tm = MIX_TM
    assert seq_len % tm == 0 and tm % SEQ_CHUNK == 0 and CONV_HALO >= CONV_KERNEL - 1
    cpb = tm // SEQ_CHUNK
    n_chunks = n // SEQ_CHUNK
    vec = pl.BlockSpec((1, cw), lambda i: (0, 0))
    col = lambda rows: pl.BlockSpec((rows, tm), lambda i: (0, i))
    const = lambda shape: pl.BlockSpec(shape, lambda i: (0, 0), pipeline_mode=pl.Buffered(1))
    dw_rows = jnp.broadcast_to(dw[:, None, :], (CONV_KERNEL, SUBLANES, cw))
    return pl.pallas_call(
        functools.partial(_mixer_in_kernel, blocks_per_seq=seq_len // tm),
        grid=(n // tm,),
        in_specs=[pl.BlockSpec((tm, d), lambda i: (i, 0)),
                  pl.BlockSpec((1, tm), lambda i: (0, i)),
                  pl.BlockSpec((2 * cw, d), lambda i: (0, 0), pipeline_mode=pl.Buffered(1)),
                  pl.BlockSpec((pl.Element(_ROWS), pl.Element(d)),
                               lambda i: (pl.multiple_of(i * 0 + 2 * cw, 2 * SUBLANES), 0),
                               pipeline_mode=pl.Buffered(1)),
                  pl.BlockSpec(fq.shape, lambda i: (0, 0)),
                  pl.BlockSpec(fi.shape, lambda i: (0, 0)),
                  pl.BlockSpec((CONV_KERNEL, SUBLANES, cw), lambda i: (0, 0, 0)),
                  vec, vec, vec],
        out_specs=[pl.BlockSpec((tm, cw), lambda i: (i, 0)),
                   col(ATTN_WIDTH), col(IDX_HEADS * IDX_DIM), col(IDX_HEADS),
                   pl.BlockSpec((cpb, SEQ_CHUNK, KV_WIDTH), lambda i: (i, 0, 0)),
                   pl.BlockSpec((cpb, KV_WIDTH, SEQ_CHUNK), lambda i: (i, 0, 0)),
                   pl.BlockSpec((cpb, SEQ_CHUNK, LANES), lambda i: (i, 0, 0))],
        out_shape=[jax.ShapeDtypeStruct((n, cw), BF16),
                   jax.ShapeDtypeStruct((ATTN_WIDTH, n), BF16),
                   jax.ShapeDtypeStruct((IDX_HEADS * IDX_DIM, n), BF16),
                   jax.ShapeDtypeStruct((IDX_HEADS, n), F32),
                   jax.ShapeDtypeStruct((n_chunks, SEQ_CHUNK, KV_WIDTH), BF16),
                   jax.ShapeDtypeStruct((n_chunks, KV_WIDTH, SEQ_CHUNK), BF16),
                   jax.ShapeDtypeStruct((n_chunks, SEQ_CHUNK, LANES), BF16)],
        scratch_shapes=[pltpu.VMEM((SUBLANES, tm + CONV_HALO, cw), F32),
                        pltpu.VMEM((tm, cw), F32)],
        compiler_params=_cparams(("arbitrary",)),
        name="mixer_inputs",
    )(h, pos_row, wtb, wtb, fq, fi, dw_rows, dwb, lng, lnb)


_ROW_Q = 0
_ROW_K = _ROW_Q + ATTN_WIDTH
_ROW_V = _ROW_K + KV_WIDTH
_ROW_QI = _ROW_V + KV_WIDTH
_ROW_KI = _ROW_QI + IDX_HEADS * IDX_DIM
_ROW_WI = _ROW_KI + IDX_DIM
_ROWS = _ROW_WI + IDX_HEADS


def _rope_rows(x, cos, sin, half):
    x1, x2 = x[0:half], x[half:2 * half]
    return jnp.concatenate([x1 * cos - x2 * sin, x2 * cos + x1 * sin, x[2 * half:]], axis=0)


def _proj_stages(h_ref, pos_ref, wt_ref, fq_ref, fi_ref, qT_ref, qiT_ref, wiT_ref, kc_ref, vT_ref, kic_ref):
    tm = h_ref.shape[0]
    n_chunk = tm // SEQ_CHUNK
    hq = HEAD_DIM // ROPE_FRACTION_DIV // 2
    hi = IDX_DIM // ROPE_FRACTION_DIV // 2

    def rot(f_ref):
        ang = f_ref[...] * pos_ref[...].astype(F32)
        return jnp.cos(ang), jnp.sin(ang)

    def proj(lo, hi_):
        return lax.dot_general(wt_ref[lo:hi_, :], h_ref[...], _NT, preferred_element_type=F32)

    def queries():
        cq, sq = rot(fq_ref)
        zq = proj(_ROW_Q, _ROW_K)
        for hd in range(N_HEADS):
            b = hd * HEAD_DIM
            qT_ref[b:b + HEAD_DIM, :] = _rope_rows(zq[b:b + HEAD_DIM], cq, sq, hq).astype(BF16)

    def keys_values():
        cq, sq = rot(fq_ref)
        zk = proj(_ROW_K, _ROW_V)
        kT = jnp.concatenate(
            [_rope_rows(zk[g * HEAD_DIM:(g + 1) * HEAD_DIM], cq, sq, hq) for g in range(N_KV_HEADS)], axis=0)
        kn = kT.T
        zv = proj(_ROW_V, _ROW_QI)
        for c in range(n_chunk):
            kc_ref[c] = kn[c * SEQ_CHUNK:(c + 1) * SEQ_CHUNK, :].astype(BF16)
            vT_ref[c] = zv[:, c * SEQ_CHUNK:(c + 1) * SEQ_CHUNK].astype(BF16)

    def indexer_queries():
        ci, si = rot(fi_ref)
        zqi = proj(_ROW_QI, _ROW_KI)
        for hd in range(IDX_HEADS):
            b = hd * IDX_DIM
            qiT_ref[b:b + IDX_DIM, :] = _rope_rows(zqi[b:b + IDX_DIM], ci, si, hi).astype(BF16)

    def indexer_keys():
        ci, si = rot(fi_ref)
        zr = proj(_ROW_KI, _ROWS)
        kiT = jnp.concatenate(
            [_rope_rows(zr[0:IDX_DIM], ci, si, hi), jnp.zeros((LANES - IDX_DIM, tm), F32)], axis=0)
        kin = kiT.T
        for c in range(n_chunk):
            kic_ref[c] = kin[c * SEQ_CHUNK:(c + 1) * SEQ_CHUNK, :].astype(BF16)
        wiT_ref[...] = zr[IDX_DIM:IDX_DIM + IDX_HEADS] * (IDX_HEADS ** -0.5 * IDX_DIM ** -0.5)

    return [queries, keys_values, indexer_queries, indexer_keys]


FOLD_ROWS = 2 * SUBLANES


def _fold(x):
    return x.reshape(x.shape[0] // FOLD_ROWS, FOLD_ROWS, x.shape[1])


def _dsa_kernel(qT_ref, qiT_ref, wiT_ref, kc_ref, vT_ref, kic_ref, o_ref, sc_ref, lg_ref, oacc_ref, *,
                topk, idx_steps):
    sc_len, tq = sc_ref.shape[1], sc_ref.shape[2]
    i = pl.program_id(1)
    n_vis = (i * tq) // sc_len + 1
    rep = N_HEADS // N_KV_HEADS
    neg_inf = jnp.float32(-jnp.inf)
    row = lax.broadcasted_iota(jnp.int32, (sc_len, tq), 0)
    lane = lax.broadcasted_iota(jnp.int32, (sc_len, tq), 1)
    rel = row - lane
    w = wiT_ref[...]

    def score_body(c, carry):
        mx8, mn8 = carry
        kch = kic_ref[c][:, :IDX_DIM]
        acc = jnp.zeros((sc_len, tq), F32)
        for h in range(IDX_HEADS):
            d = _dot(kch, qiT_ref[h * IDX_DIM:(h + 1) * IDX_DIM, :])
            acc = acc + w[h:h + 1, :] * jnp.maximum(d, 0.0)
        causal = rel <= i * tq - c * sc_len
        s = jnp.where(causal, acc, neg_inf)
        sc_ref[c] = s
        mx8 = jnp.maximum(mx8, _fold(s).max(axis=0))
        mn8 = jnp.minimum(mn8, _fold(jnp.where(causal, acc, jnp.inf)).min(axis=0))
        return mx8, mn8

    mx8, mn8 = lax.fori_loop(
        0, n_vis, score_body,
        (jnp.full((FOLD_ROWS, tq), -jnp.inf, F32), jnp.full((FOLD_ROWS, tq), jnp.inf, F32)))
    row_max = mx8.max(axis=0, keepdims=True)
    row_min = mn8.min(axis=0, keepdims=True)

    def count(pred):
        def body(c, cnt8):
            return cnt8 + _fold(jnp.where(pred(c, sc_ref[c]), 1.0, 0.0)).sum(axis=0)
        cnt8 = lax.fori_loop(0, n_vis, body, jnp.zeros((FOLD_ROWS, tq), F32))
        return cnt8.sum(axis=0, keepdims=True)

    def count_ge(t):
        return count(lambda c, s: s >= t)

    n_valid = (i * tq + lane[0:1, :] + 1).astype(F32)
    kk = jnp.minimum(n_valid, float(topk))
    cnt_top = count_ge(row_max)
    top_is_enough = cnt_top >= kk
    lo = jnp.where(top_is_enough, row_max, row_min)
    cnt_lo = jnp.where(top_is_enough, cnt_top, n_valid)
    hi = row_max
    cnt_hi = jnp.where(top_is_enough, 0.0, cnt_top)

    def active(lo, hi, cnt_lo):
        mid = 0.5 * lo + 0.5 * hi
        return (cnt_lo != kk) & (mid > lo) & (mid < hi)

    def any_lane(mask):
        return jnp.max(jnp.where(mask, 1, 0))

    def w_body(carry):
        lo, hi, cnt_lo, cnt_hi, _ = carry
        for _ in range(BISECT_STEPS):
            mid = 0.5 * lo + 0.5 * hi
            act = active(lo, hi, cnt_lo)
            cnt = count_ge(mid)
            up = act & (cnt >= kk)
            down = act & (cnt < kk)
            lo = jnp.where(up, mid, lo)
            cnt_lo = jnp.where(up, cnt, cnt_lo)
            hi = jnp.where(down, mid, hi)
            cnt_hi = jnp.where(down, cnt, cnt_hi)
        return lo, hi, cnt_lo, cnt_hi, any_lane(active(lo, hi, cnt_lo))

    lo, hi, cnt_lo, cnt_hi, _ = lax.while_loop(
        lambda carry: carry[4] > 0, w_body, (lo, hi, cnt_lo, cnt_hi, any_lane(active(lo, hi, cnt_lo))))
    thr = lo

    tie = cnt_lo > kk

    @pl.when(any_lane(tie) > 0)
    def _():
        need = kk - cnt_hi

        def key_index(c):
            return (c * sc_len + row).astype(F32)

        def j_body(_, carry):
            j_lo, j_hi = carry
            mid = jnp.floor(0.5 * (j_lo + j_hi))
            enough = count(lambda c, s: (s == thr) & (key_index(c) <= mid)) >= need
            return jnp.where(enough, j_lo, mid), jnp.where(enough, mid, j_hi)

        _, j_hi = lax.fori_loop(0, idx_steps, j_body, (jnp.full((1, tq), -1.0, F32), n_valid - 1.0))
        j_keep = jnp.where(tie, j_hi, jnp.inf)

        def drop_body(c, carry):
            s = sc_ref[c]
            sc_ref[c] = jnp.where((s == thr) & (key_index(c) > j_keep), neg_inf, s)
            return carry

        lax.fori_loop(0, n_vis, drop_body, 0)

    c_exp = HEAD_DIM ** -0.5 * math.log2(math.e)
    for first in range(0, N_HEADS, ATT_HEADS):
        g = first // rep
        heads = range(first, first + ATT_HEADS)

        def logits_body(c, m8s):
            kch = kc_ref[c][:, g * HEAD_DIM:(g + 1) * HEAD_DIM]
            sel = sc_ref[c] >= thr
            out = []
            for r, hd in enumerate(heads):
                l = _dot(kch, qT_ref[hd * HEAD_DIM:(hd + 1) * HEAD_DIM, :])
                l = jnp.where(sel, l, neg_inf)
                lg_ref[r, c] = l
                out.append(jnp.maximum(m8s[r], _fold(l).max(axis=0)))
            return tuple(out)

        m8s = lax.fori_loop(0, n_vis, logits_body,
                            tuple(jnp.full((FOLD_ROWS, tq), -jnp.inf, F32) for _ in heads))
        ms = [m8.max(axis=0, keepdims=True) for m8 in m8s]
        oacc_ref[...] = jnp.zeros(oacc_ref.shape, F32)

        def pv_body(c, s8s):
            vch = vT_ref[c][g * HEAD_DIM:(g + 1) * HEAD_DIM, :]
            out = []
            for r in range(ATT_HEADS):
                p = jnp.exp2((lg_ref[r, c] - ms[r]) * c_exp)
                oacc_ref[r] += _dot(vch, p.astype(BF16))
                out.append(s8s[r] + _fold(p).sum(axis=0))
            return tuple(out)

        s8s = lax.fori_loop(0, n_vis, pv_body, tuple(jnp.zeros((FOLD_ROWS, tq), F32) for _ in heads))
        for r, hd in enumerate(heads):
            den = s8s[r].sum(axis=0, keepdims=True)
            o_ref[:, hd * HEAD_DIM:(hd + 1) * HEAD_DIM] = (oacc_ref[r] / den).T.astype(BF16)


def _sparse_attention(qT, qiT, wiT, kc, vT, kic, batch, seq_len):
    n = qT.shape[1]
    tq = Q_BLOCK
    assert seq_len % SEQ_CHUNK == 0 and SEQ_CHUNK % tq == 0
    n_q = seq_len // tq
    n_c = seq_len // SEQ_CHUNK
    topk = min(TOPK_MAX, seq_len // 4)
    idx_steps = math.ceil(math.log2(seq_len)) + 1
    qcol = lambda rows: pl.BlockSpec((rows, tq), lambda b, i: (0, b * n_q + i))
    seq = lambda r, c: pl.BlockSpec((n_c, r, c), lambda b, i: (b, 0, 0))
    return pl.pallas_call(
        functools.partial(_dsa_kernel, topk=topk, idx_steps=idx_steps),
        grid=(batch, n_q),
        in_specs=[qcol(ATTN_WIDTH), qcol(IDX_HEADS * IDX_DIM), qcol(IDX_HEADS),
                  seq(SEQ_CHUNK, KV_WIDTH), seq(KV_WIDTH, SEQ_CHUNK), seq(SEQ_CHUNK, LANES)],
        out_specs=pl.BlockSpec((tq, ATTN_WIDTH), lambda b, i: (b * n_q + i, 0)),
        out_shape=jax.ShapeDtypeStruct((n, ATTN_WIDTH), BF16),
        scratch_shapes=[pltpu.VMEM((n_c, SEQ_CHUNK, tq), F32),
                        pltpu.VMEM((ATT_HEADS, n_c, SEQ_CHUNK, tq), F32),
                        pltpu.VMEM((ATT_HEADS, HEAD_DIM, tq), F32)],
        compiler_params=_cparams(("parallel", "arbitrary")),
        name="dsa_attention",
    )(qT, qiT, wiT, kc, vT, kic)


def _gate_merge_kernel(h_ref, c_ref, o_ref, wg0_ref, wg1_ref, wpw_ref, wo_ref, m_ref):
    h = h_ref[...]
    gate = lambda w_ref: jax.nn.sigmoid(
        lax.dot_general(h, w_ref[0].astype(BF16), _NT, preferred_element_type=F32))
    g0 = gate(wg0_ref)
    g1 = gate(wg1_ref)
    y_conv = _dot(c_ref[...], wpw_ref[...].astype(BF16))
    y_attn = _dot(o_ref[...], wo_ref[...].astype(BF16))
    m_ref[...] = (g0 * y_conv + g1 * y_attn).astype(BF16)


def _gate_merge(h, c, o, w_t, layer, gate_row, wpw, wo):
    assert gate_row % SUBLANES == 0
    n, d = h.shape
    tm, tn = MERGE_TM, MERGE_TN
    nj = d // tn
    row = lambda width: pl.BlockSpec((tm, width), lambda j, i: (i, 0))
    colw = lambda rows: pl.BlockSpec((rows, tn), lambda j, i: (0, j))
    gate = lambda branch: pl.BlockSpec(
        (pl.Element(1), pl.Element(tn), pl.Element(d)),
        lambda j, i: (layer, pl.multiple_of(gate_row + branch * d + j * tn, SUBLANES), 0))
    return pl.pallas_call(
        _gate_merge_kernel,
        grid=(nj, n // tm),
        in_specs=[row(d), row(c.shape[1]), row(o.shape[1]),
                  gate(0), gate(1),
                  colw(wpw.shape[0]), colw(wo.shape[0])],
        out_specs=pl.BlockSpec((tm, tn), lambda j, i: (i, j)),
        out_shape=jax.ShapeDtypeStruct((n, d), BF16),
        compiler_params=_cparams(("parallel", "parallel")),
        name="gate_merge",
    )(h, c, o, w_t, w_t, wpw, wo)


def _out_proj_kernel(x_ref, m_ref, wout_ref, gpost_ref, out_ref):
    y = _dot(m_ref[...], wout_ref[...].astype(BF16))
    out_ref[...] = x_ref[...] + _rms(y, gpost_ref[...])


def _out_proj(x, merged, wout, gpost):
    n, d = x.shape
    tm = OUT_TM
    row = pl.BlockSpec((tm, d), lambda i: (i, 0))
    return pl.pallas_call(
        _out_proj_kernel,
        grid=(n // tm,),
        in_specs=[row, row,
                  pl.BlockSpec((d, d), lambda i: (0, 0), pipeline_mode=pl.Buffered(1)),
                  pl.BlockSpec((1, d), lambda i: (0, 0))],
        out_specs=row,
        out_shape=jax.ShapeDtypeStruct((n, d), F32),
        compiler_params=_cparams(("parallel",)),
        name="out_proj",
    )(x, merged, wout, gpost)


def _cast_rows_kernel(w_ref, o_ref):
    o_ref[...] = w_ref[...].astype(BF16)


def _cast_rows(w_t, layer, n_rows):
    _, rows, d = w_t.shape
    tr = CAST_ROWS
    n_blocks = pl.cdiv(n_rows, tr)
    assert n_blocks * tr <= rows
    return pl.pallas_call(
        _cast_rows_kernel,
        grid=(n_blocks,),
        in_specs=[pl.BlockSpec((None, tr, d), lambda i: (layer, i, 0))],
        out_specs=pl.BlockSpec((tr, d), lambda i: (i, 0)),
        out_shape=jax.ShapeDtypeStruct((n_blocks * tr, d), BF16),
        compiler_params=_cparams(("parallel",)),
        name="cast_rows",
    )(w_t)


def _inv_freq(rot_dims):
    half = rot_dims // 2
    return jnp.power(jnp.float32(ROPE_THETA), -jnp.arange(half, dtype=jnp.float32) * (2.0 / rot_dims))


def kernel(x, positions, ffn1_norm_pre, ffn1_w1, ffn1_w2, ffn1_norm_post, mix_norm_pre, w_in, conv_dw, conv_dw_b, conv_ln_g, conv_ln_b, conv_w_pw, attn_w_o, w_out, mix_norm_post, ffn2_norm_pre, ffn2_w1, ffn2_w2, ffn2_norm_post):
    batch, seq_len, d = x.shape
    depth = ffn1_w1.shape[0]
    n = batch * seq_len
    xf = x.reshape(n, d)
    pos_row = positions.reshape(1, n)
    w_t = jnp.swapaxes(w_in, 1, 2)
    fq = _inv_freq(HEAD_DIM // ROPE_FRACTION_DIV).reshape(-1, 1)
    fi = _inv_freq(IDX_DIM // ROPE_FRACTION_DIV).reshape(-1, 1)

    o_q = 2 * CONV_WIDTH
    o_k = o_q + ATTN_WIDTH
    o_v = o_k + KV_WIDTH
    o_qi = o_v + KV_WIDTH
    o_ki = o_qi + IDX_HEADS * IDX_DIM
    o_wi = o_ki + IDX_DIM
    o_g = o_wi + IDX_HEADS

    for l in range(depth):
        wtb = _cast_rows(w_t, l, o_g)

        xf, h = _ffn(xf, ffn1_norm_pre[l][None], ffn1_w1[l], ffn1_w2[l],
                     ffn1_norm_post[l][None], mix_norm_pre[l][None])
        c, qT, qiT, wiT, kc, vT, kic = _mixer_inputs(
            h, pos_row, wtb, fq, fi, conv_dw[l], conv_dw_b[l][None], conv_ln_g[l][None], conv_ln_b[l][None],
            seq_len)
        o = _sparse_attention(qT, qiT, wiT, kc, vT, kic, batch, seq_len)
        merged = _gate_merge(h, c, o, w_t, l, o_g, conv_w_pw[l], attn_w_o[l])
        xf = _out_proj(xf, merged, w_out[l], mix_norm_post[l][None])
        xf = _ffn(xf, ffn2_norm_pre[l][None], ffn2_w1[l], ffn2_w2[l], ffn2_norm_post[l][None])
    return xf.reshape(batch, seq_len, d)
```

```python
import functools
import math

import jax
import jax.numpy as jnp
from jax import lax
from jax.experimental import pallas as pl
from jax.experimental.pallas import tpu as pltpu

F32 = jnp.float32
BF16 = jnp.bfloat16

CONV_WIDTH = 1024
CONV_KERNEL = 31
N_HEADS = 8
N_KV_HEADS = 2
N_BRANCHES = 2
HEAD_DIM = 128
ATTN_WIDTH = N_HEADS * HEAD_DIM
KV_WIDTH = N_KV_HEADS * HEAD_DIM
IDX_HEADS = 16
IDX_DIM = 64
TOPK_MAX = 256
ROPE_THETA = 500000.0
ROPE_FRACTION_DIV = 4
NORM_EPS = 1e-6

V7X_VMEM_BYTES = 64 * 1024 * 1024
VMEM_LIMIT_BYTES = V7X_VMEM_BYTES - 3 * 1024 * 1024

SUBLANES = 8
LANES = 128

FFN_TM = 1024
FFN_TF = 256
MIX_TM = 512
CONV_RC = 32
CONV_LW = 512
CONV_HALO = 32
CONV_LN_ROWS = 128
SEQ_CHUNK = 512
Q_BLOCK = 512
ATT_HEADS = 2
BISECT_STEPS = 4
MERGE_TM = 1024
MERGE_TN = 512
OUT_TM = 512
CAST_ROWS = 976

_NT = (((1,), (1,)), ((), ()))


def _cparams(sem):
    return pltpu.CompilerParams(dimension_semantics=sem, vmem_limit_bytes=VMEM_LIMIT_BYTES)


def _rms(x, gain):
    ms = jnp.mean(x * x, axis=-1, keepdims=True)
    return x * lax.rsqrt(ms + NORM_EPS) * gain


def _dot(a, b):
    return jnp.dot(a, b, preferred_element_type=F32)


def _ffn_kernel(*refs, nj, emit_h):
    if emit_h:
        x_ref, gpre_ref, w1g_ref, w1u_ref, w2_ref, gpost_ref, gnext_ref, o_ref, h_ref, hn_ref = refs
    else:
        x_ref, gpre_ref, w1g_ref, w1u_ref, w2_ref, gpost_ref, o_ref, hn_ref = refs
    j = pl.program_id(1)

    @pl.when(j == 0)
    def _():
        hn_ref[...] = _rms(x_ref[...], gpre_ref[...]).astype(BF16)
        o_ref[...] = jnp.zeros(o_ref.shape, F32)

    hn = hn_ref[...]
    g = _dot(hn, w1g_ref[...].astype(BF16))
    u = _dot(hn, w1u_ref[...].astype(BF16))
    a = (g * jax.nn.sigmoid(g) * u).astype(BF16)
    o_ref[...] += _dot(a, w2_ref[...].astype(BF16))

    @pl.when(j == nj - 1)
    def _():
        out = x_ref[...] + 0.5 * _rms(o_ref[...], gpost_ref[...])
        o_ref[...] = out
        if emit_h:
            h_ref[...] = _rms(out, gnext_ref[...]).astype(BF16)


def _ffn(x, gpre, w1, w2, gpost, gnext=None):
    n, d = x.shape
    f = w2.shape[0]
    tm, tf = FFN_TM, FFN_TF
    nj = f // tf
    assert n % tm == 0 and f % tf == 0
    emit_h = gnext is not None
    row = pl.BlockSpec((tm, d), lambda i, j: (i, 0))
    vec = pl.BlockSpec((1, d), lambda i, j: (0, 0))
    in_specs = [row, vec,
                pl.BlockSpec((d, tf), lambda i, j: (0, j)),
                pl.BlockSpec((d, tf), lambda i, j: (0, j + nj)),
                pl.BlockSpec((tf, d), lambda i, j: (j, 0)),
                vec]
    args = [x, gpre, w1, w1, w2, gpost]
    out_shape = [jax.ShapeDtypeStruct((n, d), F32)]
    out_specs = [row]
    if emit_h:
        in_specs.append(vec)
        args.append(gnext)
        out_shape.append(jax.ShapeDtypeStruct((n, d), BF16))
        out_specs.append(row)
    res = pl.pallas_call(
        functools.partial(_ffn_kernel, nj=nj, emit_h=emit_h),
        grid=(n // tm, nj),
        in_specs=in_specs,
        out_specs=out_specs,
        out_shape=out_shape,
        scratch_shapes=[pltpu.VMEM((tm, d), BF16)],
        compiler_params=_cparams(("parallel", "arbitrary")),
        name="ffn_emit_h" if emit_h else "ffn",
    )(*args)
    return res if emit_h else res[0]


def _mixer_in_kernel(h_ref, pos_ref, wab_ref, wt_ref, fq_ref, fi_ref, dw_ref, dwb_ref, lng_ref, lnb_ref,
                     c_ref, qT_ref, qiT_ref, wiT_ref, kc_ref, vT_ref, kic_ref, sh_ref, pre_ref, *, blocks_per_seq):
    tm = h_ref.shape[0]
    cw = c_ref.shape[1]
    halo = CONV_HALO
    i = pl.program_id(0)
    first = (i % blocks_per_seq) == 0

    @pl.when(first)
    def _():
        sh_ref[0, 0:halo, :] = jnp.zeros((halo, cw), F32)

    @pl.when(jnp.logical_not(first))
    def _():
        sh_ref[0, 0:halo, :] = sh_ref[0, tm:tm + halo, :]

    z = lax.dot_general(h_ref[...], wab_ref[...], _NT, preferred_element_type=F32)
    sh_ref[0, halo:halo + tm, :] = z[:, :cw] * jax.nn.sigmoid(z[:, cw:])

    span = tm + halo - SUBLANES
    for s in range(1, SUBLANES):
        sh_ref[s, 0:span, :] = sh_ref[0, s:s + span, :]

    first_tap = halo - (CONV_KERNEL - 1)

    def taps(r0):
        for lt in range(cw // CONV_LW):
            lanes = slice(lt * CONV_LW, (lt + 1) * CONV_LW)
            acc = jnp.zeros((CONV_RC // SUBLANES, SUBLANES, CONV_LW), F32)
            for k in range(CONV_KERNEL):
                off = first_tap + k
                s, base = off % SUBLANES, off - off % SUBLANES
                x = sh_ref[s, r0 + base:r0 + base + CONV_RC, lanes]
                acc = acc + dw_ref[k, :, lanes][None] * x.reshape(acc.shape)
            pre_ref[r0:r0 + CONV_RC, lanes] = acc.reshape(CONV_RC, CONV_LW)

    stages = _proj_stages(h_ref, pos_ref, wt_ref, fq_ref, fi_ref, qT_ref, qiT_ref, wiT_ref, kc_ref, vT_ref, kic_ref)
    chunks = list(range(0, tm, CONV_RC))
    share = len(chunks) // len(stages)
    for si, stage in enumerate(stages):
        stage()
        last = len(chunks) if si == len(stages) - 1 else (si + 1) * share
        for r0 in chunks[si * share:last]:
            taps(r0)

    def norm(rb, carry):
        r0 = pl.multiple_of(rb * CONV_LN_ROWS, CONV_LN_ROWS)
        acc = pre_ref[pl.ds(r0, CONV_LN_ROWS), :] + dwb_ref[...]
        mu = jnp.mean(acc, axis=-1, keepdims=True)
        xc = acc - mu
        var = jnp.mean(xc * xc, axis=-1, keepdims=True)
        y = xc * lax.rsqrt(var + NORM_EPS) * lng_ref[...] + lnb_ref[...]
        c_ref[pl.ds(r0, CONV_LN_ROWS), :] = (y * jax.nn.sigmoid(y)).astype(BF16)
        return carry

    lax.fori_loop(0, tm // CONV_LN_ROWS, norm, 0)


def _mixer_inputs(h, pos_row, wtb, fq, fi, dw, dwb, lng, lnb, seq_len):
    n, d = h.shape
    cw = dw.shape[1]
    tm = MIX_TM
    assert seq_len % tm == 0 and tm % SEQ_CHUNK == 0 and CONV_HALO >= CONV_KERNEL - 1
    cpb = tm // SEQ_CHUNK
    n_chunks = n // SEQ_CHUNK
    vec = pl.BlockSpec((1, cw), lambda i: (0, 0))
    col = lambda rows: pl.BlockSpec((rows, tm), lambda i: (0, i))
    const = lambda shape: pl.BlockSpec(shape, lambda i: (0, 0), pipeline_mode=pl.Buffered(1))
    dw_rows = jnp.broadcast_to(dw[:, None, :], (CONV_KERNEL, SUBLANES, cw))
    return pl.pallas_call(
        functools.partial(_mixer_in_kernel, blocks_per_seq=seq_len // tm),
        grid=(n // tm,),
        in_specs=[pl.BlockSpec((tm, d), lambda i: (i, 0)),
                  pl.BlockSpec((1, tm), lambda i: (0, i)),
                  pl.BlockSpec((2 * cw, d), lambda i: (0, 0), pipeline_mode=pl.Buffered(1)),
                  pl.BlockSpec((pl.Element(_ROWS), pl.Element(d)),
                               lambda i: (pl.multiple_of(i * 0 + 2 * cw, 2 * SUBLANES), 0),
                               pipeline_mode=pl.Buffered(1)),
                  pl.BlockSpec(fq.shape, lambda i: (0, 0)),
                  pl.BlockSpec(fi.shape, lambda i: (0, 0)),
                  pl.BlockSpec((CONV_KERNEL, SUBLANES, cw), lambda i: (0, 0, 0)),
                  vec, vec, vec],
        out_specs=[pl.BlockSpec((tm, cw), lambda i: (i, 0)),
                   col(ATTN_WIDTH), col(IDX_HEADS * IDX_DIM), col(IDX_HEADS),
                   pl.BlockSpec((cpb, SEQ_CHUNK, KV_WIDTH), lambda i: (i, 0, 0)),
                   pl.BlockSpec((cpb, KV_WIDTH, SEQ_CHUNK), lambda i: (i, 0, 0)),
                   pl.BlockSpec((cpb, SEQ_CHUNK, LANES), lambda i: (i, 0, 0))],
        out_shape=[jax.ShapeDtypeStruct((n, cw), BF16),
                   jax.ShapeDtypeStruct((ATTN_WIDTH, n), BF16),
                   jax.ShapeDtypeStruct((IDX_HEADS * IDX_DIM, n), BF16),
                   jax.ShapeDtypeStruct((IDX_HEADS, n), F32),
                   jax.ShapeDtypeStruct((n_chunks, SEQ_CHUNK, KV_WIDTH), BF16),
                   jax.ShapeDtypeStruct((n_chunks, KV_WIDTH, SEQ_CHUNK), BF16),
                   jax.ShapeDtypeStruct((n_chunks, SEQ_CHUNK, LANES), BF16)],
        scratch_shapes=[pltpu.VMEM((SUBLANES, tm + CONV_HALO, cw), F32),
                        pltpu.VMEM((tm, cw), F32)],
        compiler_params=_cparams(("arbitrary",)),
        name="mixer_inputs",
    )(h, pos_row, wtb, wtb, fq, fi, dw_rows, dwb, lng, lnb)


_ROW_Q = 0
_ROW_K = _ROW_Q + ATTN_WIDTH
_ROW_V = _ROW_K + KV_WIDTH
_ROW_QI = _ROW_V + KV_WIDTH
_ROW_KI = _ROW_QI + IDX_HEADS * IDX_DIM
_ROW_WI = _ROW_KI + IDX_DIM
_ROWS = _ROW_WI + IDX_HEADS


def _rope_rows(x, cos, sin, half):
    x1, x2 = x[0:half], x[half:2 * half]
    return jnp.concatenate([x1 * cos - x2 * sin, x2 * cos + x1 * sin, x[2 * half:]], axis=0)


def _proj_stages(h_ref, pos_ref, wt_ref, fq_ref, fi_ref, qT_ref, qiT_ref, wiT_ref, kc_ref, vT_ref, kic_ref):
    tm = h_ref.shape[0]
    n_chunk = tm // SEQ_CHUNK
    hq = HEAD_DIM // ROPE_FRACTION_DIV // 2
    hi = IDX_DIM // ROPE_FRACTION_DIV // 2

    def rot(f_ref):
        ang = f_ref[...] * pos_ref[...].astype(F32)
        return jnp.cos(ang), jnp.sin(ang)

    def proj(lo, hi_):
        return lax.dot_general(wt_ref[lo:hi_, :], h_ref[...], _NT, preferred_element_type=F32)

    def queries():
        cq, sq = rot(fq_ref)
        zq = proj(_ROW_Q, _ROW_K)
        for hd in range(N_HEADS):
            b = hd * HEAD_DIM
            qT_ref[b:b + HEAD_DIM, :] = _rope_rows(zq[b:b + HEAD_DIM], cq, sq, hq).astype(BF16)

    def keys_values():
        cq, sq = rot(fq_ref)
        zk = proj(_ROW_K, _ROW_V)
        kT = jnp.concatenate(
            [_rope_rows(zk[g * HEAD_DIM:(g + 1) * HEAD_DIM], cq, sq, hq) for g in range(N_KV_HEADS)], axis=0)
        kn = kT.T
        zv = proj(_ROW_V, _ROW_QI)
        for c in range(n_chunk):
            kc_ref[c] = kn[c * SEQ_CHUNK:(c + 1) * SEQ_CHUNK, :].astype(BF16)
            vT_ref[c] = zv[:, c * SEQ_CHUNK:(c + 1) * SEQ_CHUNK].astype(BF16)

    def indexer_queries():
        ci, si = rot(fi_ref)
        zqi = proj(_ROW_QI, _ROW_KI)
        for hd in range(IDX_HEADS):
            b = hd * IDX_DIM
            qiT_ref[b:b + IDX_DIM, :] = _rope_rows(zqi[b:b + IDX_DIM], ci, si, hi).astype(BF16)

    def indexer_keys():
        ci, si = rot(fi_ref)
        zr = proj(_ROW_KI, _ROWS)
        kiT = jnp.concatenate(
            [_rope_rows(zr[0:IDX_DIM], ci, si, hi), jnp.zeros((LANES - IDX_DIM, tm), F32)], axis=0)
        kin = kiT.T
        for c in range(n_chunk):
            kic_ref[c] = kin[c * SEQ_CHUNK:(c + 1) * SEQ_CHUNK, :].astype(BF16)
        wiT_ref[...] = zr[IDX_DIM:IDX_DIM + IDX_HEADS] * (IDX_HEADS ** -0.5 * IDX_DIM ** -0.5)

    return [queries, keys_values, indexer_queries, indexer_keys]


FOLD_ROWS = 2 * SUBLANES


def _fold(x):
    return x.reshape(x.shape[0] // FOLD_ROWS, FOLD_ROWS, x.shape[1])


def _dsa_kernel(qT_ref, qiT_ref, wiT_ref, kc_ref, vT_ref, kic_ref, o_ref, sc_ref, lg_ref, oacc_ref, *,
                topk, idx_steps):
    sc_len, tq = sc_ref.shape[1], sc_ref.shape[2]
    i = pl.program_id(1)
    n_vis = (i * tq) // sc_len + 1
    rep = N_HEADS // N_KV_HEADS
    neg_inf = jnp.float32(-jnp.inf)
    row = lax.broadcasted_iota(jnp.int32, (sc_len, tq), 0)
    lane = lax.broadcasted_iota(jnp.int32, (sc_len, tq), 1)
    rel = row - lane
    w = wiT_ref[...]

    def score_body(c, carry):
        mx8, mn8 = carry
        kch = kic_ref[c][:, :IDX_DIM]
        acc = jnp.zeros((sc_len, tq), F32)
        for h in range(IDX_HEADS):
            d = _dot(kch, qiT_ref[h * IDX_DIM:(h + 1) * IDX_DIM, :])
            acc = acc + w[h:h + 1, :] * jnp.maximum(d, 0.0)
        causal = rel <= i * tq - c * sc_len
        s = jnp.where(causal, acc, neg_inf)
        sc_ref[c] = s
        mx8 = jnp.maximum(mx8, _fold(s).max(axis=0))
        mn8 = jnp.minimum(mn8, _fold(jnp.where(causal, acc, jnp.inf)).min(axis=0))
        return mx8, mn8

    mx8, mn8 = lax.fori_loop(
        0, n_vis, score_body,
        (jnp.full((FOLD_ROWS, tq), -jnp.inf, F32), jnp.full((FOLD_ROWS, tq), jnp.inf, F32)))
    row_max = mx8.max(axis=0, keepdims=True)
    row_min = mn8.min(axis=0, keepdims=True)

    def count(pred):
        def body(c, cnt8):
            return cnt8 + _fold(jnp.where(pred(c, sc_ref[c]), 1.0, 0.0)).sum(axis=0)
        cnt8 = lax.fori_loop(0, n_vis, body, jnp.zeros((FOLD_ROWS, tq), F32))
        return cnt8.sum(axis=0, keepdims=True)

    def count_ge(t):
        return count(lambda c, s: s >= t)

    n_valid = (i * tq + lane[0:1, :] + 1).astype(F32)
    kk = jnp.minimum(n_valid, float(topk))
    cnt_top = count_ge(row_max)
    top_is_enough = cnt_top >= kk
    lo = jnp.where(top_is_enough, row_max, row_min)
    cnt_lo = jnp.where(top_is_enough, cnt_top, n_valid)
    hi = row_max
    cnt_hi = jnp.where(top_is_enough, 0.0, cnt_top)

    def active(lo, hi, cnt_lo):
        mid = 0.5 * lo + 0.5 * hi
        return (cnt_lo != kk) & (mid > lo) & (mid < hi)

    def any_lane(mask):
        return jnp.max(jnp.where(mask, 1, 0))

    def w_body(carry):
        lo, hi, cnt_lo, cnt_hi, _ = carry
        for _ in range(BISECT_STEPS):
            mid = 0.5 * lo + 0.5 * hi
            act = active(lo, hi, cnt_lo)
            cnt = count_ge(mid)
            up = act & (cnt >= kk)
            down = act & (cnt < kk)
            lo = jnp.where(up, mid, lo)
            cnt_lo = jnp.where(up, cnt, cnt_lo)
            hi = jnp.where(down, mid, hi)
            cnt_hi = jnp.where(down, cnt, cnt_hi)
        return lo, hi, cnt_lo, cnt_hi, any_lane(active(lo, hi, cnt_lo))

    lo, hi, cnt_lo, cnt_hi, _ = lax.while_loop(
        lambda carry: carry[4] > 0, w_body, (lo, hi, cnt_lo, cnt_hi, any_lane(active(lo, hi, cnt_lo))))
    thr = lo

    tie = cnt_lo > kk

    @pl.when(any_lane(tie) > 0)
    def _():
        need = kk - cnt_hi

        def key_index(c):
            return (c * sc_len + row).astype(F32)

        def j_body(_, carry):
            j_lo, j_hi = carry
            mid = jnp.floor(0.5 * (j_lo + j_hi))
            enough = count(lambda c, s: (s == thr) & (key_index(c) <= mid)) >= need
            return jnp.where(enough, j_lo, mid), jnp.where(enough, mid, j_hi)

        _, j_hi = lax.fori_loop(0, idx_steps, j_body, (jnp.full((1, tq), -1.0, F32), n_valid - 1.0))
        j_keep = jnp.where(tie, j_hi, jnp.inf)

        def drop_body(c, carry):
            s = sc_ref[c]
            sc_ref[c] = jnp.where((s == thr) & (key_index(c) > j_keep), neg_inf, s)
            return carry

        lax.fori_loop(0, n_vis, drop_body, 0)

    c_exp = HEAD_DIM ** -0.5 * math.log2(math.e)
    for first in range(0, N_HEADS, ATT_HEADS):
        g = first // rep
        heads = range(first, first + ATT_HEADS)

        def logits_body(c, m8s):
            kch = kc_ref[c][:, g * HEAD_DIM:(g + 1) * HEAD_DIM]
            sel = sc_ref[c] >= thr
            out = []
            for r, hd in enumerate(heads):
                l = _dot(kch, qT_ref[hd * HEAD_DIM:(hd + 1) * HEAD_DIM, :])
                l = jnp.where(sel, l, neg_inf)
                lg_ref[r, c] = l
                out.append(jnp.maximum(m8s[r], _fold(l).max(axis=0)))
            return tuple(out)

        m8s = lax.fori_loop(0, n_vis, logits_body,
                            tuple(jnp.full((FOLD_ROWS, tq), -jnp.inf, F32) for _ in heads))
        ms = [m8.max(axis=0, keepdims=True) for m8 in m8s]
        oacc_ref[...] = jnp.zeros(oacc_ref.shape, F32)

        def pv_body(c, s8s):
            vch = vT_ref[c][g * HEAD_DIM:(g + 1) * HEAD_DIM, :]
            out = []
            for r in range(ATT_HEADS):
                p = jnp.exp2((lg_ref[r, c] - ms[r]) * c_exp)
                oacc_ref[r] += _dot(vch, p.astype(BF16))
                out.append(s8s[r] + _fold(p).sum(axis=0))
            return tuple(out)

        s8s = lax.fori_loop(0, n_vis, pv_body, tuple(jnp.zeros((FOLD_ROWS, tq), F32) for _ in heads))
        for r, hd in enumerate(heads):
            den = s8s[r].sum(axis=0, keepdims=True)
            o_ref[:, hd * HEAD_DIM:(hd + 1) * HEAD_DIM] = (oacc_ref[r] / den).T.astype(BF16)


def _sparse_attention(qT, qiT, wiT, kc, vT, kic, batch, seq_len):
    n = qT.shape[1]
    tq = Q_BLOCK
    assert seq_len % SEQ_CHUNK == 0 and SEQ_CHUNK % tq == 0
    n_q = seq_len // tq
    n_c = seq_len // SEQ_CHUNK
    topk = min(TOPK_MAX, seq_len // 4)
    idx_steps = math.ceil(math.log2(seq_len)) + 1
    qcol = lambda rows: pl.BlockSpec((rows, tq), lambda b, i: (0, b * n_q + i))
    seq = lambda r, c: pl.BlockSpec((n_c, r, c), lambda b, i: (b, 0, 0))
    return pl.pallas_call(
        functools.partial(_dsa_kernel, topk=topk, idx_steps=idx_steps),
        grid=(batch, n_q),
        in_specs=[qcol(ATTN_WIDTH), qcol(IDX_HEADS * IDX_DIM), qcol(IDX_HEADS),
                  seq(SEQ_CHUNK, KV_WIDTH), seq(KV_WIDTH, SEQ_CHUNK), seq(SEQ_CHUNK, LANES)],
        out_specs=pl.BlockSpec((tq, ATTN_WIDTH), lambda b, i: (b * n_q + i, 0)),
        out_shape=jax.ShapeDtypeStruct((n, ATTN_WIDTH), BF16),
        scratch_shapes=[pltpu.VMEM((n_c, SEQ_CHUNK, tq), F32),
                        pltpu.VMEM((ATT_HEADS, n_c, SEQ_CHUNK, tq), F32),
                        pltpu.VMEM((ATT_HEADS, HEAD_DIM, tq), F32)],
        compiler_params=_cparams(("parallel", "arbitrary")),
        name="dsa_attention",
    )(qT, qiT, wiT, kc, vT, kic)


def _gate_merge_kernel(h_ref, c_ref, o_ref, wg0_ref, wg1_ref, wpw_ref, wo_ref, m_ref):
    h = h_ref[...]
    gate = lambda w_ref: jax.nn.sigmoid(
        lax.dot_general(h, w_ref[0].astype(BF16), _NT, preferred_element_type=F32))
    g0 = gate(wg0_ref)
    g1 = gate(wg1_ref)
    y_conv = _dot(c_ref[...], wpw_ref[...].astype(BF16))
    y_attn = _dot(o_ref[...], wo_ref[...].astype(BF16))
    m_ref[...] = (g0 * y_conv + g1 * y_attn).astype(BF16)


def _gate_merge(h, c, o, w_t, layer, gate_row, wpw, wo):
    assert gate_row % SUBLANES == 0
    n, d = h.shape
    tm, tn = MERGE_TM, MERGE_TN
    nj = d // tn
    row = lambda width: pl.BlockSpec((tm, width), lambda j, i: (i, 0))
    colw = lambda rows: pl.BlockSpec((rows, tn), lambda j, i: (0, j))
    gate = lambda branch: pl.BlockSpec(
        (pl.Element(1), pl.Element(tn), pl.Element(d)),
        lambda j, i: (layer, pl.multiple_of(gate_row + branch * d + j * tn, SUBLANES), 0))
    return pl.pallas_call(
        _gate_merge_kernel,
        grid=(nj, n // tm),
        in_specs=[row(d), row(c.shape[1]), row(o.shape[1]),
                  gate(0), gate(1),
                  colw(wpw.shape[0]), colw(wo.shape[0])],
        out_specs=pl.BlockSpec((tm, tn), lambda j, i: (i, j)),
        out_shape=jax.ShapeDtypeStruct((n, d), BF16),
        compiler_params=_cparams(("parallel", "parallel")),
        name="gate_merge",
    )(h, c, o, w_t, w_t, wpw, wo)


def _out_proj_kernel(x_ref, m_ref, wout_ref, gpost_ref, out_ref):
    y = _dot(m_ref[...], wout_ref[...].astype(BF16))
    out_ref[...] = x_ref[...] + _rms(y, gpost_ref[...])


def _out_proj(x, merged, wout, gpost):
    n, d = x.shape
    tm = OUT_TM
    row = pl.BlockSpec((tm, d), lambda i: (i, 0))
    return pl.pallas_call(
        _out_proj_kernel,
        grid=(n // tm,),
        in_specs=[row, row,
                  pl.BlockSpec((d, d), lambda i: (0, 0), pipeline_mode=pl.Buffered(1)),
                  pl.BlockSpec((1, d), lambda i: (0, 0))],
        out_specs=row,
        out_shape=jax.ShapeDtypeStruct((n, d), F32),
        compiler_params=_cparams(("parallel",)),
        name="out_proj",
    )(x, merged, wout, gpost)


def _cast_rows_kernel(w_ref, o_ref):
    o_ref[...] = w_ref[...].astype(BF16)


def _cast_rows(w_t, layer, n_rows):
    _, rows, d = w_t.shape
    tr = CAST_ROWS
    n_blocks = pl.cdiv(n_rows, tr)
    assert n_blocks * tr <= rows
    return pl.pallas_call(
        _cast_rows_kernel,
        grid=(n_blocks,),
        in_specs=[pl.BlockSpec((None, tr, d), lambda i: (layer, i, 0))],
        out_specs=pl.BlockSpec((tr, d), lambda i: (i, 0)),
        out_shape=jax.ShapeDtypeStruct((n_blocks * tr, d), BF16),
        compiler_params=_cparams(("parallel",)),
        name="cast_rows",
    )(w_t)


def _inv_freq(rot_dims):
    half = rot_dims // 2
    return jnp.power(jnp.float32(ROPE_THETA), -jnp.arange(half, dtype=jnp.float32) * (2.0 / rot_dims))


def kernel(x, positions, ffn1_norm_pre, ffn1_w1, ffn1_w2, ffn1_norm_post, mix_norm_pre, w_in, conv_dw, conv_dw_b, conv_ln_g, conv_ln_b, conv_w_pw, attn_w_o, w_out, mix_norm_post, ffn2_norm_pre, ffn2_w1, ffn2_w2, ffn2_norm_post):
    batch, seq_len, d = x.shape
    depth = ffn1_w1.shape[0]
    n = batch * seq_len
    xf = x.reshape(n, d)
    pos_row = positions.reshape(1, n)
    w_t = jnp.swapaxes(w_in, 1, 2)
    fq = _inv_freq(HEAD_DIM // ROPE_FRACTION_DIV).reshape(-1, 1)
    fi = _inv_freq(IDX_DIM // ROPE_FRACTION_DIV).reshape(-1, 1)

    o_q = 2 * CONV_WIDTH
    o_k = o_q + ATTN_WIDTH
    o_v = o_k + KV_WIDTH
    o_qi = o_v + KV_WIDTH
    o_ki = o_qi + IDX_HEADS * IDX_DIM
    o_wi = o_ki + IDX_DIM
    o_g = o_wi + IDX_HEADS
    assert w_in.shape[2] == o_g + N_BRANCHES * d, "unexpected column layout of the combined input projection"

    for l in range(depth):
        wtb = _cast_rows(w_t, l, o_g)

        xf, h = _ffn(xf, ffn1_norm_pre[l][None], ffn1_w1[l], ffn1_w2[l],
                     ffn1_norm_post[l][None], mix_norm_pre[l][None])
        c, qT, qiT, wiT, kc, vT, kic = _mixer_inputs(
            h, pos_row, wtb, fq, fi, conv_dw[l], conv_dw_b[l][None], conv_ln_g[l][None], conv_ln_b[l][None],
            seq_len)
        o = _sparse_attention(qT, qiT, wiT, kc, vT, kic, batch, seq_len)
        merged = _gate_merge(h, c, o, w_t, l, o_g, conv_w_pw[l], attn_w_o[l])
        xf = _out_proj(xf, merged, w_out[l], mix_norm_post[l][None])
        xf = _ffn(xf, ffn2_norm_pre[l][None], ffn2_w1[l], ffn2_w2[l], ffn2_norm_post[l][None])
    return xf.reshape(batch, seq_len, d)
```
